```python
import jax, jax.numpy as jnp
from jax import lax
import numpy as np

D_MODEL = 2048
BATCH = 2
SEQ = 16384
DEPTH = 2

HEAD_DIM = 64
ROPE_THETA = 10000.0
RMS_EPS = 1e-6
NEG_INF = -1e30

POOL_WINDOWS = (2, 4, 8, 16)
POOL_GROUPS = 4
POOL_WIDTH = D_MODEL // 4
POOL_GC = POOL_WIDTH // POOL_GROUPS

SWA_HQ = 16
SWA_HKV = 2
SWA_REP = SWA_HQ // SWA_HKV
SWA_WINDOW = 128
SWA_BLOCK = 128
SWA_WIDTH = SWA_HQ * HEAD_DIM
SWA_KV_WIDTH = SWA_HKV * HEAD_DIM

MOBA_HEADS = 8
MOBA_BLOCK = 256
MOBA_TOPK = 3
MOBA_Q_CHUNK = 64
MOBA_WIDTH = MOBA_HEADS * HEAD_DIM

N_BRANCHES = 3
W_IN_COLS = POOL_WIDTH + SWA_WIDTH + 2 * SWA_KV_WIDTH + 3 * MOBA_WIDTH + N_BRANCHES * D_MODEL

N_GROUPS = 4
EXPERTS_PER_GROUP = 8
N_EXPERTS = N_GROUPS * EXPERTS_PER_GROUP
MOE_TOPK = 2
EXPERT_FF = D_MODEL // 2
MOE_BLOCK = 256

kernel_name = 'hybrid_pool_swa_moba_hmoe'


def rms_norm(x, gain):
    xf = x.astype(jnp.float32)
    y = xf * lax.rsqrt(jnp.mean(xf * xf, axis=-1, keepdims=True) + RMS_EPS)
    return y.astype(x.dtype) * gain


def rope_tables(seq_len):
    inv_freq = 1.0 / (ROPE_THETA ** (jnp.arange(0, HEAD_DIM, 2, dtype=jnp.float32) / HEAD_DIM))
    ang = jnp.arange(seq_len, dtype=jnp.float32)[:, None] * inv_freq[None, :]
    return jnp.cos(ang), jnp.sin(ang)


def apply_rope(x, cos, sin):
    xf = x.astype(jnp.float32)
    x1, x2 = jnp.split(xf, 2, axis=-1)
    c = cos[None, :, None, :]
    s = sin[None, :, None, :]
    return jnp.concatenate([x1 * c - x2 * s, x2 * c + x1 * s], axis=-1).astype(x.dtype)


def pool_mixer(xa, pool_w, pool_scale):
    B, S, _ = xa.shape
    xf = xa.astype(jnp.float32)
    csum = jnp.concatenate([jnp.zeros((B, 1, POOL_WIDTH), jnp.float32), jnp.cumsum(xf, axis=1)], axis=1)
    pos = jnp.arange(S)
    outs = []
    for g, w in enumerate(POOL_WINDOWS):
        sl = slice(g * POOL_GC, (g + 1) * POOL_GC)
        lo = jnp.maximum(pos + 1 - w, 0)
        c = csum[..., sl]
        win_sum = jnp.take(c, pos + 1, axis=1) - jnp.take(c, lo, axis=1)
        cnt = (pos + 1 - lo).astype(jnp.float32)[None, :, None]
        pooled = (win_sum / cnt - xf[..., sl]).astype(xa.dtype)
        outs.append(jnp.einsum('bsc,ce->bse', pooled, pool_w[g]))
    return jnp.concatenate(outs, axis=-1) * pool_scale


def shift_prev_block(t):
    pad = [(0, 0), (1, 0)] + [(0, 0)] * (t.ndim - 2)
    return jnp.pad(t, pad)[:, :t.shape[1]]


def swa_attention(q, k, v, sinks):
    B, S, _, Dh = q.shape
    nb = S // SWA_BLOCK
    scale = HEAD_DIM ** -0.5
    qb = q.reshape(B, nb, SWA_BLOCK, SWA_HKV, SWA_REP, Dh)
    kb = k.reshape(B, nb, SWA_BLOCK, SWA_HKV, Dh)
    vb = v.reshape(B, nb, SWA_BLOCK, SWA_HKV, Dh)
    kk = jnp.concatenate([shift_prev_block(kb), kb], axis=2)
    vv = jnp.concatenate([shift_prev_block(vb), vb], axis=2)
    s = jnp.einsum('bnqgrd,bnkgd->bngrqk', qb, kk).astype(jnp.float32) * scale
    qi = jnp.arange(SWA_BLOCK)[:, None]
    kj = jnp.arange(2 * SWA_BLOCK)[None, :]
    dist = qi + SWA_BLOCK - kj
    band = (dist >= 0) & (dist < SWA_WINDOW)
    mask = band[None] & ((jnp.arange(nb)[:, None, None] > 0) | (kj[None] >= SWA_BLOCK))
    s = jnp.where(mask[None, :, None, None], s, NEG_INF)
    sink = sinks.astype(jnp.float32).reshape(1, 1, SWA_HKV, SWA_REP, 1, 1)
    m = jnp.maximum(jnp.max(s, axis=-1, keepdims=True), sink)
    e = jnp.exp(s - m)
    p = (e / (jnp.sum(e, axis=-1, keepdims=True) + jnp.exp(sink - m))).astype(v.dtype)
    o = jnp.einsum('bngrqk,bnkgd->bnqgrd', p, vv)
    return o.reshape(B, S, SWA_HQ * Dh)


def moba_attention(q, k, v):
    B, S, H, Dh = q.shape
    nb = -(-S // MOBA_BLOCK)
    sp = nb * MOBA_BLOCK
    pad = ((0, 0), (0, sp - S), (0, 0), (0, 0))
    q, k, v = jnp.pad(q, pad), jnp.pad(k, pad), jnp.pad(v, pad)
    scale = HEAD_DIM ** -0.5
    k_blocks = k.reshape(B, nb, MOBA_BLOCK, H, Dh)
    v_blocks = v.reshape(B, nb, MOBA_BLOCK, H, Dh)
    k_mean = jnp.mean(k_blocks.astype(jnp.float32), axis=2)
    gate = jnp.einsum('bshd,bnhd->bshn', q.astype(jnp.float32), k_mean)
    q_blk = jnp.arange(sp) // MOBA_BLOCK
    past = jnp.arange(nb)[None, :] < q_blk[:, None]
    gate = jnp.where(past[None, :, None, :], gate, -jnp.inf)
    topk = min(MOBA_TOPK, nb)
    _, sel = lax.top_k(gate, topk)
    sel_ok = sel < q_blk[None, :, None, None]
    k_bh = k_blocks.transpose(0, 3, 1, 2, 4)
    v_bh = v_blocks.transpose(0, 3, 1, 2, 4)
    b_idx = jnp.arange(B)[:, None, None, None]
    h_idx = jnp.arange(H)[None, None, :, None]

    def chunk(c):
        start = c * MOBA_Q_CHUNK
        qc = lax.dynamic_slice_in_dim(q, start, MOBA_Q_CHUNK, axis=1)
        sel_c = lax.dynamic_slice_in_dim(sel, start, MOBA_Q_CHUNK, axis=1)
        ok_c = lax.dynamic_slice_in_dim(sel_ok, start, MOBA_Q_CHUNK, axis=1)
        k_sel = k_bh[b_idx, h_idx, sel_c]
        v_sel = v_bh[b_idx, h_idx, sel_c]
        s_sel = jnp.einsum('bqhd,bqhnkd->bqhnk', qc, k_sel).astype(jnp.float32) * scale
        s_sel = jnp.where(ok_c[..., None], s_sel, NEG_INF).reshape(B, MOBA_Q_CHUNK, H, topk * MOBA_BLOCK)
        own = (start // MOBA_BLOCK) * MOBA_BLOCK
        k_own = lax.dynamic_slice_in_dim(k, own, MOBA_BLOCK, axis=1)
        v_own = lax.dynamic_slice_in_dim(v, own, MOBA_BLOCK, axis=1)
        s_own = jnp.einsum('bqhd,bkhd->bqhk', qc, k_own).astype(jnp.float32) * scale
        causal = (own + jnp.arange(MOBA_BLOCK))[None, :] <= (start + jnp.arange(MOBA_Q_CHUNK))[:, None]
        s_own = jnp.where(causal[None, :, None, :], s_own, NEG_INF)
        p = jax.nn.softmax(jnp.concatenate([s_sel, s_own], axis=-1), axis=-1).astype(v.dtype)
        p_sel = p[..., :topk * MOBA_BLOCK].reshape(B, MOBA_Q_CHUNK, H, topk, MOBA_BLOCK)
        p_own = p[..., topk * MOBA_BLOCK:]
        return (jnp.einsum('bqhnk,bqhnkd->bqhd', p_sel, v_sel)
                + jnp.einsum('bqhk,bkhd->bqhd', p_own, v_own))

    out = lax.map(chunk, jnp.arange(sp // MOBA_Q_CHUNK))
    out = jnp.moveaxis(out, 0, 1).reshape(B, sp, H * Dh)
    return out[:, :S]


def hier_moe(h, w_router_group, w_router_expert, w_exp_gate, w_exp_up, w_exp_down):
    B, S, D = h.shape
    T = B * S
    hf = h.reshape(T, D)
    grp_logits = jnp.einsum('td,dg->tg', hf, w_router_group).astype(jnp.float32)
    grp_prob = jax.nn.softmax(grp_logits, axis=-1)
    g_sel = jnp.argmax(grp_logits, axis=-1).astype(jnp.int32)
    p_grp = jnp.take_along_axis(grp_prob, g_sel[:, None], axis=-1)
    exp_logits = jnp.einsum('td,de->te', hf, w_router_expert).astype(jnp.float32).reshape(T, N_GROUPS, EXPERTS_PER_GROUP)
    within = jnp.take_along_axis(exp_logits, g_sel[:, None, None], axis=1)[:, 0]
    top_p, top_e = lax.top_k(jax.nn.softmax(within, axis=-1), MOE_TOPK)
    weights = p_grp * top_p / jnp.sum(top_p, axis=-1, keepdims=True)
    expert_id = (g_sel[:, None] * EXPERTS_PER_GROUP + top_e).astype(jnp.int32)

    n_assign = T * MOE_TOPK
    flat_e = expert_id.reshape(-1)
    flat_t = jnp.repeat(jnp.arange(T, dtype=jnp.int32), MOE_TOPK)
    flat_w = weights.reshape(-1)
    e_sorted, t_sorted, w_sorted = lax.sort((flat_e, flat_t, flat_w), num_keys=1, is_stable=True)
    counts = jnp.bincount(flat_e, length=N_EXPERTS)
    starts = jnp.cumsum(counts) - counts
    padded = ((counts + MOE_BLOCK - 1) // MOE_BLOCK) * MOE_BLOCK
    p_ends = jnp.cumsum(padded)
    p_starts = p_ends - padded
    rank = jnp.arange(n_assign, dtype=jnp.int32) - starts[e_sorted]
    dest = p_starts[e_sorted] + rank
    n_blocks = -(-n_assign // MOE_BLOCK) + N_EXPERTS
    n_slots = n_blocks * MOE_BLOCK
    slot_tok = jnp.full((n_slots,), T, jnp.int32).at[dest].set(t_sorted)
    slot_w = jnp.zeros((n_slots,), jnp.float32).at[dest].set(w_sorted)
    blk_exp = jnp.minimum(jnp.searchsorted(p_ends, jnp.arange(n_blocks) * MOE_BLOCK, side='right'),
                          N_EXPERTS - 1).astype(jnp.int32)
    hf_pad = jnp.concatenate([hf, jnp.zeros((1, D), hf.dtype)], axis=0)

    def run_block(args):
        tok, wt, e = args
        xb = hf_pad[tok]
        act = jax.nn.silu(xb @ w_exp_gate[e]) * (xb @ w_exp_up[e])
        return (act @ w_exp_down[e]) * wt[:, None].astype(hf.dtype)

    ys = lax.map(run_block, (slot_tok.reshape(n_blocks, MOE_BLOCK), slot_w.reshape(n_blocks, MOE_BLOCK), blk_exp))
    out = jnp.zeros((T + 1, D), hf.dtype).at[slot_tok].add(ys.reshape(n_slots, D))[:T]
    return out.reshape(B, S, D)


def setup_inputs(seed: int = 0) -> dict:
    key = jax.random.key(seed)
    ks = jax.random.split(key, 20)
    f32 = jnp.float32

    def nrm(k, shape, scale):
        return jax.random.normal(k, shape, f32) * scale

    return {
        'x': nrm(ks[0], (BATCH, SEQ, D_MODEL), 1.0),
        'norm_mix': 1.0 + nrm(ks[1], (DEPTH, D_MODEL), 0.02),
        'w_in': nrm(ks[2], (DEPTH, D_MODEL, W_IN_COLS), D_MODEL ** -0.5),
        'pool_w': nrm(ks[3], (DEPTH, POOL_GROUPS, POOL_GC, POOL_GC), POOL_GC ** -0.5),
        'pool_scale': 1.0 + nrm(ks[4], (DEPTH, POOL_WIDTH), 0.02),
        'swa_q_norm': 1.0 + nrm(ks[5], (DEPTH, HEAD_DIM), 0.02),
        'swa_k_norm': 1.0 + nrm(ks[6], (DEPTH, HEAD_DIM), 0.02),
        'swa_sinks': nrm(ks[7], (DEPTH, SWA_HQ), 0.5),
        'moba_q_norm': 1.0 + nrm(ks[8], (DEPTH, HEAD_DIM), 0.02),
        'moba_k_norm': 1.0 + nrm(ks[9], (DEPTH, HEAD_DIM), 0.02),
        'w_up_a': nrm(ks[10], (DEPTH, POOL_WIDTH, D_MODEL), POOL_WIDTH ** -0.5),
        'w_up_b': nrm(ks[11], (DEPTH, SWA_WIDTH, D_MODEL), SWA_WIDTH ** -0.5),
        'w_up_c': nrm(ks[12], (DEPTH, MOBA_WIDTH, D_MODEL), MOBA_WIDTH ** -0.5),
        'w_o': nrm(ks[13], (DEPTH, D_MODEL, D_MODEL), D_MODEL ** -0.5),
        'norm_ffn': 1.0 + nrm(ks[14], (DEPTH, D_MODEL), 0.02),
        'w_router_group': nrm(ks[15], (DEPTH, D_MODEL, N_GROUPS), D_MODEL ** -0.5),
        'w_router_expert': nrm(ks[16], (DEPTH, D_MODEL, N_EXPERTS), D_MODEL ** -0.5),
        'w_exp_gate': nrm(ks[17], (DEPTH, N_EXPERTS, D_MODEL, EXPERT_FF), D_MODEL ** -0.5),
        'w_exp_up': nrm(ks[18], (DEPTH, N_EXPERTS, D_MODEL, EXPERT_FF), D_MODEL ** -0.5),
        'w_exp_down': nrm(ks[19], (DEPTH, N_EXPERTS, EXPERT_FF, D_MODEL), EXPERT_FF ** -0.5),
    }


def reference(x, norm_mix, w_in, pool_w, pool_scale, swa_q_norm, swa_k_norm, swa_sinks,
              moba_q_norm, moba_k_norm, w_up_a, w_up_b, w_up_c, w_o, norm_ffn,
              w_router_group, w_router_expert, w_exp_gate, w_exp_up, w_exp_down):
    B, S, _ = x.shape
    cos, sin = rope_tables(S)
    offsets = np.cumsum([POOL_WIDTH, SWA_WIDTH, SWA_KV_WIDTH, SWA_KV_WIDTH,
                         MOBA_WIDTH, MOBA_WIDTH, MOBA_WIDTH]).tolist()
    for l in range(DEPTH):
        h = rms_norm(x, norm_mix[l])
        proj = jnp.einsum('bsd,dc->bsc', h, w_in[l])
        xa, qb, kb, vb, qm, km, vm, gate_logits = jnp.split(proj, offsets, axis=-1)
        y_a = pool_mixer(xa, pool_w[l], pool_scale[l])
        q_b = apply_rope(rms_norm(qb.reshape(B, S, SWA_HQ, HEAD_DIM), swa_q_norm[l]), cos, sin)
        k_b = apply_rope(rms_norm(kb.reshape(B, S, SWA_HKV, HEAD_DIM), swa_k_norm[l]), cos, sin)
        y_b = swa_attention(q_b, k_b, vb.reshape(B, S, SWA_HKV, HEAD_DIM), swa_sinks[l])
        q_c = apply_rope(rms_norm(qm.reshape(B, S, MOBA_HEADS, HEAD_DIM), moba_q_norm[l]), cos, sin)
        k_c = apply_rope(rms_norm(km.reshape(B, S, MOBA_HEADS, HEAD_DIM), moba_k_norm[l]), cos, sin)
        y_c = moba_attention(q_c, k_c, vm.reshape(B, S, MOBA_HEADS, HEAD_DIM))
        gates = jax.nn.sigmoid(gate_logits.astype(jnp.float32)).astype(x.dtype)
        g_a, g_b, g_c = jnp.split(gates, N_BRANCHES, axis=-1)
        merged = (g_a * jnp.einsum('bsc,cd->bsd', y_a, w_up_a[l])
                  + g_b * jnp.einsum('bsc,cd->bsd', y_b, w_up_b[l])
                  + g_c * jnp.einsum('bsc,cd->bsd', y_c, w_up_c[l]))
        x = x + jnp.einsum('bsd,de->bse', merged, w_o[l])
        x = x + hier_moe(rms_norm(x, norm_ffn[l]), w_router_group[l], w_router_expert[l],
                         w_exp_gate[l], w_exp_up[l], w_exp_down[l])
    return x
```

```python
import functools

import jax
import jax.numpy as jnp
from jax import lax
from jax.experimental import pallas as pl
from jax.experimental.pallas import tpu as pltpu

F32 = jnp.float32
BF16 = jnp.bfloat16

HEAD_DIM = 64
ROPE_THETA = 10000.0
RMS_EPS = 1e-6
NEG_INF = -1e30

POOL_WINDOWS = (2, 4, 8, 16)
POOL_GC = 128

SWA_HQ = 16
SWA_HKV = 2
SWA_BLOCK = 128

MOBA_HEADS = 8
MOBA_BLOCK = 256
MOBA_TOPK = 3

N_GROUPS = 4
EXPERTS_PER_GROUP = 8
N_EXPERTS = 32
MOE_BLOCK = 256

LANES = 128
VMEM_LIMIT = 56 * 1024 * 1024

_NT = (((1,), (1,)), ((), ()))


def _params(n_axes, vmem=VMEM_LIMIT):
    return pltpu.CompilerParams(dimension_semantics=("arbitrary",) * n_axes, vmem_limit_bytes=vmem)


def _rmsnorm_kernel(x_ref, g_ref, o_ref):
    x = x_ref[...]
    ms = jnp.mean(x * x, axis=-1, keepdims=True)
    o_ref[...] = (x * lax.rsqrt(ms + RMS_EPS) * g_ref[...]).astype(o_ref.dtype)


def rmsnorm_cast(x2d, gain, tm=512):
    t, d = x2d.shape
    return pl.pallas_call(
        _rmsnorm_kernel,
        grid=(t // tm,),
        in_specs=[pl.BlockSpec((tm, d), lambda i: (i, 0)), pl.BlockSpec((1, d), lambda i: (0, 0))],
        out_specs=pl.BlockSpec((tm, d), lambda i: (i, 0)),
        out_shape=jax.ShapeDtypeStruct((t, d), BF16),
        compiler_params=_params(1),
        name="rmsnorm",
    )(x2d, gain.reshape(1, d))


def _matmul_kernel(a_ref, w_ref, o_ref):
    o_ref[...] = jnp.dot(a_ref[...], w_ref[...], preferred_element_type=F32).astype(o_ref.dtype)


def in_proj(h, w, tm=2048, tn=256):
    t, k = h.shape
    n = w.shape[1]
    tm = min(tm, t)
    return pl.pallas_call(
        _matmul_kernel,
        grid=(t // tm, n // tn),
        in_specs=[pl.BlockSpec((tm, k), lambda i, j: (i, 0)), pl.BlockSpec((k, tn), lambda i, j: (0, j))],
        out_specs=pl.BlockSpec((tm, tn), lambda i, j: (i, j)),
        out_shape=jax.ShapeDtypeStruct((t, n), BF16),
        compiler_params=_params(2),
        name="in_proj",
    )(h, w)


def _norm_rope(x, gain, cos, sin_signed, blockdiag, first_half):
    x2 = x * x
    hi = x2.astype(BF16)
    lo = (x2 - hi.astype(F32)).astype(BF16)
    ss = jnp.dot(hi, blockdiag, preferred_element_type=F32) + jnp.dot(lo, blockdiag, preferred_element_type=F32)
    y = x * lax.rsqrt(ss * (1.0 / HEAD_DIM) + RMS_EPS) * gain
    rot = jnp.where(first_half, pltpu.roll(y, LANES - HEAD_DIM // 2, 1), pltpu.roll(y, HEAD_DIM // 2, 1))
    return y * cos + rot * sin_signed


def _prep_kernel(qb_ref, kb_ref, qm_ref, km_ref, cos_ref, sin_ref, gains_ref,
                 qb_o, kb_o, qm_o, km_o, kmean_o, *, scale):
    tm = qb_ref.shape[0]
    cos = cos_ref[...]
    sin = sin_ref[...]
    lane = lax.broadcasted_iota(jnp.int32, (tm, LANES), 1)
    first_half = (lane % HEAD_DIM) < (HEAD_DIM // 2)
    r = lax.broadcasted_iota(jnp.int32, (LANES, LANES), 0) // HEAD_DIM
    c = lax.broadcasted_iota(jnp.int32, (LANES, LANES), 1) // HEAD_DIM
    blockdiag = (r == c).astype(BF16)

    def run(src, dst, gain_row, mult, n_groups):
        outs = []
        gain = gains_ref[gain_row:gain_row + 1, :]
        for g in range(n_groups):
            sl = slice(g * LANES, (g + 1) * LANES)
            y = _norm_rope(src[:, sl].astype(F32), gain, cos, sin, blockdiag, first_half)
            if mult != 1.0:
                y = y * mult
            dst[:, sl] = y.astype(dst.dtype)
            outs.append(y)
        return outs

    run(qb_ref, qb_o, 0, scale, qb_ref.shape[1] // LANES)
    run(kb_ref, kb_o, 1, 1.0, kb_ref.shape[1] // LANES)
    run(qm_ref, qm_o, 2, scale, qm_ref.shape[1] // LANES)
    kms = run(km_ref, km_o, 3, 1.0, km_ref.shape[1] // LANES)
    for g, y in enumerate(kms):
        for blk in range(tm // MOBA_BLOCK):
            s = jnp.sum(y[blk * MOBA_BLOCK:(blk + 1) * MOBA_BLOCK, :], axis=0, keepdims=True)
            kmean_o[blk, :, g * LANES:(g + 1) * LANES] = s * (1.0 / MOBA_BLOCK)


def qk_prep(proj, cols, cos_t, sin_t, gains, seq, tm=512):
    t = proj.shape[0]
    tiles_per_seq = seq // tm
    wqb, wkb, wqm, wkm = SWA_HQ * HEAD_DIM, SWA_HKV * HEAD_DIM, MOBA_HEADS * HEAD_DIM, MOBA_HEADS * HEAD_DIM

    def col_spec(width, off):
        return pl.BlockSpec((tm, width), lambda i, o=off // width: (i, o))

    tab_spec = pl.BlockSpec((tm, LANES), lambda i: (i % tiles_per_seq, 0))
    return pl.pallas_call(
        functools.partial(_prep_kernel, scale=HEAD_DIM ** -0.5),
        grid=(t // tm,),
        in_specs=[col_spec(wqb, cols["qb"]), col_spec(wkb, cols["kb"]), col_spec(wqm, cols["qm"]),
                  col_spec(wkm, cols["km"]), tab_spec, tab_spec, pl.BlockSpec((8, LANES), lambda i: (0, 0))],
        out_specs=[pl.BlockSpec((tm, wqb), lambda i: (i, 0)), pl.BlockSpec((tm, wkb), lambda i: (i, 0)),
                   pl.BlockSpec((tm, wqm), lambda i: (i, 0)), pl.BlockSpec((tm, wkm), lambda i: (i, 0)),
                   pl.BlockSpec((tm // MOBA_BLOCK, 1, wkm), lambda i: (i, 0, 0))],
        out_shape=[jax.ShapeDtypeStruct((t, wqb), BF16), jax.ShapeDtypeStruct((t, wkb), BF16),
                   jax.ShapeDtypeStruct((t, wqm), BF16), jax.ShapeDtypeStruct((t, wkm), BF16),
                   jax.ShapeDtypeStruct((t // MOBA_BLOCK, 1, wkm), F32)],
        compiler_params=_params(1),
        name="qk_prep",
    )(proj, proj, proj, proj, cos_t, sin_t, gains)


def _pool_kernel(cur_ref, prev_ref, pw_ref, ps_ref, o_ref, *, tiles_per_seq):
    tm = cur_ref.shape[0]
    i = pl.program_id(0)
    tile_in_seq = i % tiles_per_seq
    row = lax.broadcasted_iota(jnp.int32, (tm, tm), 0)
    col = lax.broadcasted_iota(jnp.int32, (tm, tm), 1)
    pos = tile_in_seq * tm + lax.broadcasted_iota(jnp.int32, (tm, 1), 0)
    has_prev = (tile_in_seq > 0).astype(F32)
    for g, w in enumerate(POOL_WINDOWS):
        sl = slice(g * POOL_GC, (g + 1) * POOL_GC)
        cur = cur_ref[:, sl]
        prev = prev_ref[:, sl]
        band_cur = ((row >= col) & (row - col < w)).astype(BF16)
        band_prev = (row + tm - col < w).astype(BF16)
        win = jnp.dot(band_cur, cur, preferred_element_type=F32)
        win = win + has_prev * jnp.dot(band_prev, prev, preferred_element_type=F32)
        cnt = jnp.minimum(pos + 1, w).astype(F32)
        pooled = (win / cnt - cur.astype(F32)).astype(BF16)
        y = jnp.dot(pooled, pw_ref[g], preferred_element_type=F32) * ps_ref[:, sl]
        o_ref[:, sl] = y.astype(o_ref.dtype)


def pool_mixer(proj, col_off, pool_w, pool_scale, seq, tm=256):
    t = proj.shape[0]
    width = len(POOL_WINDOWS) * POOL_GC
    cb = col_off // width
    tiles_per_seq = seq // tm
    return pl.pallas_call(
        functools.partial(_pool_kernel, tiles_per_seq=tiles_per_seq),
        grid=(t // tm,),
        in_specs=[pl.BlockSpec((tm, width), lambda i: (i, cb)),
                  pl.BlockSpec((tm, width), lambda i: (jnp.maximum(i - 1, 0), cb)),
                  pl.BlockSpec((len(POOL_WINDOWS), POOL_GC, POOL_GC), lambda i: (0, 0, 0)),
                  pl.BlockSpec((1, width), lambda i: (0, 0))],
        out_specs=pl.BlockSpec((tm, width), lambda i: (i, 0)),
        out_shape=jax.ShapeDtypeStruct((t, width), BF16),
        compiler_params=_params(1),
        name="pool_mixer",
    )(proj, proj, pool_w.astype(BF16), pool_scale.reshape(1, width))


def _half_select(x, lane, low):
    keep = (lane < HEAD_DIM) if low else (lane >= HEAD_DIM)
    return jnp.where(keep, x, 0.0)


def _swa_kernel(q_ref, kc_ref, kp_ref, vc_ref, vp_ref, sink_ref, o_ref, *, blocks_per_seq):
    blk = SWA_BLOCK
    i = pl.program_id(0)
    not_first = (i % blocks_per_seq) > 0
    lane = lax.broadcasted_iota(jnp.int32, (2 * blk, LANES), 1)

    def group_mats(cur_ref, prev_ref):
        x = jnp.concatenate([prev_ref[...], cur_ref[...]], axis=0).astype(F32)
        g0_lo = _half_select(x, lane, True)
        g1_hi = _half_select(x, lane, False)
        g0_hi = pltpu.roll(g0_lo, HEAD_DIM, 1)
        g1_lo = pltpu.roll(g1_hi, HEAD_DIM, 1)
        return ((g0_lo.astype(BF16), g0_hi.astype(BF16)), (g1_lo.astype(BF16), g1_hi.astype(BF16)))

    kmats = group_mats(kc_ref, kp_ref)
    vmats = group_mats(vc_ref, vp_ref)
    qi = lax.broadcasted_iota(jnp.int32, (blk, 2 * blk), 0)
    kj = lax.broadcasted_iota(jnp.int32, (blk, 2 * blk), 1)
    dist = qi + blk - kj
    band = (dist >= 0) & (dist < blk) & (not_first | (kj >= blk))
    rep = SWA_HQ // SWA_HKV
    for p in range(SWA_HQ // 2):
        grp = (2 * p) // rep
        qp = q_ref[:, p * LANES:(p + 1) * LANES]
        out = jnp.zeros((blk, LANES), F32)
        for par in range(2):
            s = lax.dot_general(qp, kmats[grp][par], _NT, preferred_element_type=F32)
            s = jnp.where(band, s, NEG_INF)
            sink = sink_ref[0, 2 * p + par]
            m = jnp.maximum(jnp.max(s, axis=1, keepdims=True), sink)
            e = jnp.exp(s - m)
            den = jnp.sum(e, axis=1, keepdims=True) + jnp.exp(sink - m)
            pr = (e / den).astype(BF16)
            out = out + jnp.dot(pr, vmats[grp][par], preferred_element_type=F32)
        o_ref[:, p * LANES:(p + 1) * LANES] = out.astype(o_ref.dtype)


def swa_attention(q, k, proj, v_col, sinks, seq):
    t = q.shape[0]
    blk = SWA_BLOCK
    bps = seq // blk
    kvw = SWA_HKV * HEAD_DIM
    vb = v_col // kvw

    def prev(i):
        return jnp.maximum(i - 1, 0)

    return pl.pallas_call(
        functools.partial(_swa_kernel, blocks_per_seq=bps),
        grid=(t // blk,),
        in_specs=[pl.BlockSpec((blk, q.shape[1]), lambda i: (i, 0)),
                  pl.BlockSpec((blk, kvw), lambda i: (i, 0)),
                  pl.BlockSpec((blk, kvw), lambda i: (prev(i), 0)),
                  pl.BlockSpec((blk, kvw), lambda i: (i, vb)),
                  pl.BlockSpec((blk, kvw), lambda i: (prev(i), vb)),
                  pl.BlockSpec((1, SWA_HQ), lambda i: (0, 0), memory_space=pltpu.SMEM)],
        out_specs=pl.BlockSpec((blk, q.shape[1]), lambda i: (i, 0)),
        out_shape=jax.ShapeDtypeStruct(q.shape, BF16),
        compiler_params=_params(1),
        name="swa_attention",
    )(q, k, k, proj, proj, sinks.reshape(1, SWA_HQ).astype(F32))


def _moba_kernel(q_ref, k_ref, v_ref, kme_ref, kmo_ref, o_ref, vt_ref, *, nblk):
    blk = MOBA_BLOCK
    i = pl.program_id(2)

    @pl.when(i == 0)
    def _():
        def tbody(j, c):
            off = pl.multiple_of(j * blk, blk)
            vt_ref[:, pl.ds(off, blk)] = v_ref[pl.ds(off, blk), :].astype(F32).T.astype(BF16)
            return c
        lax.fori_loop(0, nblk, tbody, 0)

    qp = q_ref[...]
    lane = lax.broadcasted_iota(jnp.int32, (blk, LANES), 1)
    low = lane < HEAD_DIM

    def top_bias(gates, block_of_lane, valid_lane):
        g = jnp.where(valid_lane & (block_of_lane < i), gates, -jnp.inf)
        sel = jnp.zeros(g.shape, jnp.bool_)
        for _ in range(MOBA_TOPK):
            mx = jnp.max(g, axis=1, keepdims=True)
            idx = jnp.min(jnp.where(g == mx, block_of_lane, 1 << 20), axis=1, keepdims=True)
            hit = (block_of_lane == idx) & valid_lane & (mx > -jnp.inf)
            sel = sel | hit
            g = jnp.where(hit, -jnp.inf, g)
        return jnp.where(sel, 0.0, NEG_INF)

    gates_e = lax.dot_general(qp, kme_ref[...], _NT, preferred_element_type=F32)
    gates_o = lax.dot_general(qp, kmo_ref[...], _NT, preferred_element_type=F32)
    bias_e = top_bias(gates_e, lane - HEAD_DIM, ~low)
    bias_o = top_bias(gates_o, lane, low)
    qf = qp.astype(F32)
    q_ext = (jnp.where(low, qf, bias_e).astype(BF16),
             jnp.where(low, bias_o, qf).astype(BF16))
    q_own = (jnp.where(low, qf, 0.0).astype(BF16), jnp.where(low, 0.0, qf).astype(BF16))

    def attend(st, vt_h, m, l, acc):
        m_new = jnp.maximum(m, jnp.max(st, axis=0, keepdims=True))
        alpha = jnp.exp(m - m_new)
        p = jnp.exp(st - m_new)
        l_new = l * alpha + jnp.sum(p, axis=0, keepdims=True)
        acc_new = acc * alpha + jnp.dot(vt_h, p.astype(BF16), preferred_element_type=F32)
        return m_new, l_new, acc_new

    def body(j, carry):
        off = pl.multiple_of(j * blk, blk)
        kf = k_ref[pl.ds(off, blk), :].astype(F32)
        vt = vt_ref[:, pl.ds(off, blk)]
        k_ext = (jnp.where(low, kf, (lane - HEAD_DIM == j).astype(F32)).astype(BF16),
                 jnp.where(low, (lane == j).astype(F32), kf).astype(BF16))
        out = []
        for h in range(2):
            m, l, acc = carry[h]
            st = lax.dot_general(k_ext[h], q_ext[h], _NT, preferred_element_type=F32)
            out.append(attend(st, vt[h * HEAD_DIM:(h + 1) * HEAD_DIM, :], m, l, acc))
        return tuple(out)

    init = tuple((jnp.full((1, blk), -jnp.inf, F32), jnp.zeros((1, blk), F32), jnp.zeros((HEAD_DIM, blk), F32))
                 for _ in range(2))
    carry = lax.fori_loop(0, i, body, init)

    off = pl.multiple_of(i * blk, blk)
    k_own = k_ref[pl.ds(off, blk), :]
    vt = vt_ref[:, pl.ds(off, blk)]
    key_idx = lax.broadcasted_iota(jnp.int32, (blk, blk), 0)
    qry_idx = lax.broadcasted_iota(jnp.int32, (blk, blk), 1)
    outs = []
    for h in range(2):
        m, l, acc = carry[h]
        st = lax.dot_general(k_own, q_own[h], _NT, preferred_element_type=F32)
        st = jnp.where(key_idx <= qry_idx, st, NEG_INF)
        m, l, acc = attend(st, vt[h * HEAD_DIM:(h + 1) * HEAD_DIM, :], m, l, acc)
        outs.append(acc / l)
    o_ref[...] = jnp.concatenate(outs, axis=0).T.astype(o_ref.dtype)


def moba_attention(q, k, proj, v_col, kme, kmo, batch, seq):
    t, width = q.shape
    blk = MOBA_BLOCK
    nblk = seq // blk
    pairs = width // LANES
    q3 = q.reshape(batch, seq, width)
    k3 = k.reshape(batch, seq, width)
    p3 = proj.reshape(batch, seq, proj.shape[1])
    vb = v_col // LANES
    out = pl.pallas_call(
        functools.partial(_moba_kernel, nblk=nblk),
        grid=(batch, pairs, nblk),
        in_specs=[pl.BlockSpec((None, blk, LANES), lambda b, p, i: (b, i, p)),
                  pl.BlockSpec((None, seq, LANES), lambda b, p, i: (b, 0, p)),
                  pl.BlockSpec((None, seq, LANES), lambda b, p, i: (b, 0, vb + p)),
                  pl.BlockSpec((None, None, LANES, LANES), lambda b, p, i: (b, p, 0, 0)),
                  pl.BlockSpec((None, None, LANES, LANES), lambda b, p, i: (b, p, 0, 0))],
        out_specs=pl.BlockSpec((None, blk, LANES), lambda b, p, i: (b, i, p)),
        out_shape=jax.ShapeDtypeStruct((batch, seq, width), BF16),
        scratch_shapes=[pltpu.VMEM((LANES, seq), BF16)],
        compiler_params=_params(3),
        name="moba_attention",
    )(q3, k3, p3, kme, kmo)
    return out.reshape(t, width)


def moba_gate_mats(kmean, batch, nblk):
    km = kmean.reshape(batch, nblk, MOBA_HEADS // 2, 2, HEAD_DIM).astype(BF16)
    km = jnp.transpose(km, (0, 2, 3, 1, 4))
    pad_rows = LANES // 2 - nblk
    even = jnp.pad(km[:, :, 0], ((0, 0), (0, 0), (LANES // 2, pad_rows), (0, HEAD_DIM)))
    odd = jnp.pad(km[:, :, 1], ((0, 0), (0, 0), (0, LANES // 2 + pad_rows), (HEAD_DIM, 0)))
    return even, odd


def _outproj_kernel(x_ref, ya_ref, yb_ref, yc_ref, ga_ref, gb_ref, gc_ref,
                    wa_ref, wb_ref, wc_ref, wo_ref, gain_ref, xo_ref, h_ref):
    def gate(g_ref):
        return jax.nn.sigmoid(g_ref[...].astype(F32))

    merged = gate(ga_ref) * jnp.dot(ya_ref[...], wa_ref[...], preferred_element_type=F32)
    merged = merged + gate(gb_ref) * jnp.dot(yb_ref[...], wb_ref[...], preferred_element_type=F32)
    merged = merged + gate(gc_ref) * jnp.dot(yc_ref[...], wc_ref[...], preferred_element_type=F32)
    xn = x_ref[...] + jnp.dot(merged.astype(BF16), wo_ref[...], preferred_element_type=F32)
    xo_ref[...] = xn
    ms = jnp.mean(xn * xn, axis=-1, keepdims=True)
    h_ref[...] = xn * lax.rsqrt(ms + RMS_EPS) * gain_ref[...]


def out_proj(x2d, ya, yb, yc, proj, gate_col, wa, wb, wc, wo, gain, tm=256):
    t, d = x2d.shape
    gb0 = gate_col // d

    def rows(width):
        return pl.BlockSpec((tm, width), lambda i: (i, 0))

    def whole(w):
        return pl.BlockSpec(w.shape, lambda i: (0, 0), pipeline_mode=pl.Buffered(1))

    return pl.pallas_call(
        _outproj_kernel,
        grid=(t // tm,),
        in_specs=[rows(d), rows(ya.shape[1]), rows(yb.shape[1]), rows(yc.shape[1]),
                  pl.BlockSpec((tm, d), lambda i: (i, gb0)),
                  pl.BlockSpec((tm, d), lambda i: (i, gb0 + 1)),
                  pl.BlockSpec((tm, d), lambda i: (i, gb0 + 2)),
                  whole(wa), whole(wb), whole(wc), whole(wo),
                  pl.BlockSpec((1, d), lambda i: (0, 0))],
        out_specs=[rows(d), rows(d)],
        out_shape=[jax.ShapeDtypeStruct((t, d), F32), jax.ShapeDtypeStruct((t, d), F32)],
        compiler_params=_params(1),
        name="out_proj",
    )(x2d, ya, yb, yc, proj, proj, proj, wa, wb, wc, wo, gain.reshape(1, d))


def _router_kernel(h_ref, wr_ref, route_ref, count_ref, run_ref):
    tm = h_ref.shape[0]
    epg = EXPERTS_PER_GROUP

    @pl.when(pl.program_id(0) == 0)
    def _():
        run_ref[...] = jnp.zeros(run_ref.shape, F32)

    logits = lax.dot_general(wr_ref[...], h_ref[...].astype(BF16), _NT, preferred_element_type=F32)
    row8 = lax.broadcasted_iota(jnp.int32, (epg, tm), 0)
    glog = jnp.where(row8 < N_GROUPS, logits[N_EXPERTS:N_EXPERTS + epg, :], -jnp.inf)
    gmax = jnp.max(glog, axis=0, keepdims=True)
    gsel = jnp.min(jnp.where(glog == gmax, row8, epg), axis=0, keepdims=True)
    p_grp = 1.0 / jnp.sum(jnp.exp(glog - gmax), axis=0, keepdims=True)
    within = jnp.zeros((epg, tm), F32)
    for g in range(N_GROUPS):
        within = jnp.where(gsel == g, logits[g * epg:(g + 1) * epg, :], within)
    wmax = jnp.max(within, axis=0, keepdims=True)
    e = jnp.exp(within - wmax)
    prob = e / jnp.sum(e, axis=0, keepdims=True)
    p1 = jnp.max(prob, axis=0, keepdims=True)
    i1 = jnp.min(jnp.where(prob == p1, row8, epg), axis=0, keepdims=True)
    rest = jnp.where(row8 == i1, -jnp.inf, prob)
    p2 = jnp.max(rest, axis=0, keepdims=True)
    i2 = jnp.min(jnp.where(rest == p2, row8, epg), axis=0, keepdims=True)
    norm = p1 + p2
    w1 = p_grp * p1 / norm
    w2 = p_grp * p2 / norm
    e1 = gsel * epg + i1
    e2 = gsel * epg + i2

    row32 = lax.broadcasted_iota(jnp.int32, (N_EXPERTS, tm), 0)
    oh1 = (row32 == e1).astype(F32)
    oh2 = (row32 == e2).astype(F32)
    oh = oh1 + oh2
    before = (lax.broadcasted_iota(jnp.int32, (tm, tm), 0) < lax.broadcasted_iota(jnp.int32, (tm, tm), 1)).astype(BF16)
    prior = jnp.dot(oh.astype(BF16), before, preferred_element_type=F32) + run_ref[:, 0:1]
    r1 = jnp.sum(oh1 * prior, axis=0, keepdims=True)
    r2 = jnp.sum(oh2 * prior, axis=0, keepdims=True)
    run_new = run_ref[...] + jnp.sum(oh, axis=1, keepdims=True)
    run_ref[...] = run_new
    count_ref[...] = run_new
    zero = jnp.zeros((1, tm), F32)
    route_ref[...] = jnp.concatenate([e1.astype(F32), e2.astype(F32), r1, r2, w1, w2, zero, zero], axis=0)


def router(h, wr_t, tm=512):
    t, d = h.shape
    return pl.pallas_call(
        _router_kernel,
        grid=(t // tm,),
        in_specs=[pl.BlockSpec((tm, d), lambda i: (i, 0)), pl.BlockSpec(wr_t.shape, lambda i: (0, 0))],
        out_specs=[pl.BlockSpec((8, tm), lambda i: (0, i)), pl.BlockSpec((N_EXPERTS, LANES), lambda i: (0, 0))],
        out_shape=[jax.ShapeDtypeStruct((8, t), F32), jax.ShapeDtypeStruct((N_EXPERTS, LANES), F32)],
        scratch_shapes=[pltpu.VMEM((N_EXPERTS, LANES), F32)],
        compiler_params=_params(1),
        name="router",
    )(h, wr_t)


def _row_copy(src_ref, src_row, dst_ref, dst_row, sem):
    return pltpu.make_async_copy(src_ref.at[pl.ds(src_row, 1)], dst_ref.at[pl.ds(dst_row, 1)], sem)


def _dispatch_kernel(dest_ref, h_ref, init_ref, xs_ref, sem):
    del init_ref
    tm = h_ref.shape[0]

    def start(r, c):
        for k in range(2):
            _row_copy(h_ref, r, xs_ref, dest_ref[k, r], sem).start()
        return c

    def wait(r, c):
        for k in range(2):
            _row_copy(h_ref, r, xs_ref, dest_ref[k, r], sem).wait()
        return c

    lax.fori_loop(0, tm, start, 0)
    lax.fori_loop(0, tm, wait, 0)


def dispatch(h, dest, n_slots, tm=256):
    t, d = h.shape
    dest3 = dest.reshape(2, t // tm, tm).transpose(1, 0, 2)
    return pl.pallas_call(
        _dispatch_kernel,
        grid=(t // tm,),
        in_specs=[pl.BlockSpec((None, 2, tm), lambda i: (i, 0, 0), memory_space=pltpu.SMEM),
                  pl.BlockSpec((tm, d), lambda i: (i, 0)),
                  pl.BlockSpec(memory_space=pl.ANY)],
        out_specs=pl.BlockSpec(memory_space=pl.ANY),
        out_shape=jax.ShapeDtypeStruct((n_slots, d), h.dtype),
        input_output_aliases={2: 0},
        scratch_shapes=[pltpu.SemaphoreType.DMA],
        compiler_params=_params(1),
        name="moe_dispatch",
    )(dest3, h, jnp.zeros((n_slots, d), h.dtype))


def _expert_kernel(bexp_ref, nvalid_ref, xs_ref, wg_ref, wu_ref, wd_ref, ys_ref):
    n = pl.program_id(0)
    nv = nvalid_ref[n]

    @pl.when(nv > 0)
    def _():
        row = lax.broadcasted_iota(jnp.int32, xs_ref.shape, 0)
        x = jnp.where(row < nv, xs_ref[...], 0.0).astype(BF16)
        g = jnp.dot(x, wg_ref[...], preferred_element_type=F32)
        u = jnp.dot(x, wu_ref[...], preferred_element_type=F32)
        act = (jax.nn.silu(g) * u).astype(BF16)
        ys_ref[...] = jnp.dot(act, wd_ref[...], preferred_element_type=F32)

    @pl.when(nv <= 0)
    def _():
        ys_ref[...] = jnp.zeros(ys_ref.shape, ys_ref.dtype)


def experts(xs, blk_exp, nvalid, wg, wu, wd):
    n_slots, d = xs.shape
    ff = wg.shape[2]
    nb = n_slots // MOE_BLOCK
    grid_spec = pltpu.PrefetchScalarGridSpec(
        num_scalar_prefetch=2,
        grid=(nb,),
        in_specs=[pl.BlockSpec((MOE_BLOCK, d), lambda n, be, nv: (n, 0)),
                  pl.BlockSpec((None, d, ff), lambda n, be, nv: (be[n], 0, 0)),
                  pl.BlockSpec((None, d, ff), lambda n, be, nv: (be[n], 0, 0)),
                  pl.BlockSpec((None, ff, d), lambda n, be, nv: (be[n], 0, 0))],
        out_specs=pl.BlockSpec((MOE_BLOCK, d), lambda n, be, nv: (n, 0)),
    )
    return pl.pallas_call(
        _expert_kernel,
        grid_spec=grid_spec,
        out_shape=jax.ShapeDtypeStruct((n_slots, d), F32),
        compiler_params=_params(1),
        name="moe_experts",
    )(blk_exp, nvalid, xs, wg, wu, wd)


def _combine_kernel(dest_ref, x_ref, w_ref, gain_ref, ys_ref, xo_ref, h_ref, buf_ref, sem):
    tm = x_ref.shape[0]

    def start(r, c):
        for k in range(2):
            _row_copy(ys_ref, dest_ref[k, r], buf_ref.at[k], r, sem).start()
        return c

    def wait(r, c):
        for k in range(2):
            _row_copy(ys_ref, dest_ref[k, r], buf_ref.at[k], r, sem).wait()
        return c

    lax.fori_loop(0, tm, start, 0)
    lax.fori_loop(0, tm, wait, 0)
    w = w_ref[...]
    xn = x_ref[...] + buf_ref[0] * w[:, 0:1] + buf_ref[1] * w[:, 1:2]
    xo_ref[...] = xn
    ms = jnp.mean(xn * xn, axis=-1, keepdims=True)
    h_ref[...] = (xn * lax.rsqrt(ms + RMS_EPS) * gain_ref[...]).astype(h_ref.dtype)


def combine(x2d, ys, dest, wts, gain, tm=256):
    t, d = x2d.shape
    dest3 = dest.reshape(2, t // tm, tm).transpose(1, 0, 2)
    return pl.pallas_call(
        _combine_kernel,
        grid=(t // tm,),
        in_specs=[pl.BlockSpec((None, 2, tm), lambda i: (i, 0, 0), memory_space=pltpu.SMEM),
                  pl.BlockSpec((tm, d), lambda i: (i, 0)),
                  pl.BlockSpec((tm, 2), lambda i: (i, 0)),
                  pl.BlockSpec((1, d), lambda i: (0, 0)),
                  pl.BlockSpec(memory_space=pl.ANY)],
        out_specs=[pl.BlockSpec((tm, d), lambda i: (i, 0)), pl.BlockSpec((tm, d), lambda i: (i, 0))],
        out_shape=[jax.ShapeDtypeStruct((t, d), F32), jax.ShapeDtypeStruct((t, d), BF16)],
        scratch_shapes=[pltpu.VMEM((2, tm, d), F32), pltpu.SemaphoreType.DMA],
        compiler_params=_params(1),
        name="moe_combine",
    )(dest3, x2d, wts, gain.reshape(1, d), ys)


def _rope_tables(seq):
    inv_freq = 1.0 / (ROPE_THETA ** (jnp.arange(0, HEAD_DIM, 2, dtype=F32) / HEAD_DIM))
    ang = jnp.arange(seq, dtype=F32)[:, None] * inv_freq[None, :]
    cos, sin = jnp.cos(ang), jnp.sin(ang)
    cos_t = jnp.concatenate([cos, cos, cos, cos], axis=1)
    sin_t = jnp.concatenate([-sin, sin, -sin, sin], axis=1)
    return cos_t, sin_t


def _moe_plan(route, counts, n_tokens):
    eid = route[0:2].astype(jnp.int32)
    rank = route[2:4].astype(jnp.int32)
    wts = route[4:6].T
    cnt = counts[:, 0].astype(jnp.int32)
    padded = ((cnt + MOE_BLOCK - 1) // MOE_BLOCK) * MOE_BLOCK
    p_ends = jnp.cumsum(padded)
    p_starts = p_ends - padded
    dest = p_starts[eid] + rank
    n_blocks = -(-(n_tokens * 2) // MOE_BLOCK) + N_EXPERTS
    blk_start = jnp.arange(n_blocks, dtype=jnp.int32) * MOE_BLOCK
    blk_exp = jnp.minimum(jnp.searchsorted(p_ends, blk_start, side="right"), N_EXPERTS - 1).astype(jnp.int32)
    nvalid = jnp.clip(cnt[blk_exp] - (blk_start - p_starts[blk_exp]), 0, MOE_BLOCK).astype(jnp.int32)
    return dest, wts, blk_exp, nvalid, n_blocks * MOE_BLOCK


def kernel(x, norm_mix, w_in, pool_w, pool_scale, swa_q_norm, swa_k_norm, swa_sinks, moba_q_norm, moba_k_norm,
           w_up_a, w_up_b, w_up_c, w_o, norm_ffn, w_router_group, w_router_expert, w_exp_gate, w_exp_up, w_exp_down):
    batch, seq, d = x.shape
    t = batch * seq
    depth = w_in.shape[0]
    pool_width = len(POOL_WINDOWS) * POOL_GC
    wqb, wkv, wm = SWA_HQ * HEAD_DIM, SWA_HKV * HEAD_DIM, MOBA_HEADS * HEAD_DIM

    names = ["xa", "qb", "kb", "vb", "qm", "km", "vm", "ga", "gb", "gc"]
    widths = [pool_width, wqb, wkv, wkv, wm, wm, wm, d, d, d]
    src, o = {}, 0
    for nme, w in zip(names, widths):
        src[nme] = (o, w)
        o += w
    order = ["ga", "gb", "gc", "qb", "qm", "km", "vm", "xa", "kb", "vb"]
    cols, o = {}, 0
    for nme in order:
        cols[nme] = o
        o += src[nme][1]

    cos_t, sin_t = _rope_tables(seq)
    x2d = x.reshape(t, d)
    h = rmsnorm_cast(x2d, norm_mix[0])

    for l in range(depth):
        w_in_l = jnp.concatenate([w_in[l][:, src[nme][0]:src[nme][0] + src[nme][1]] for nme in order],
                                 axis=1).astype(BF16)
        proj = in_proj(h, w_in_l)

        def tile2(g):
            return jnp.concatenate([g, g], axis=0)

        gains = jnp.stack([tile2(swa_q_norm[l]), tile2(swa_k_norm[l]), tile2(moba_q_norm[l]), tile2(moba_k_norm[l])]
                          + [jnp.zeros((LANES,), F32)] * 4)
        qb, kb, qm, km, kmean = qk_prep(proj, cols, cos_t, sin_t, gains, seq)
        y_a = pool_mixer(proj, cols["xa"], pool_w[l], pool_scale[l], seq)
        y_b = swa_attention(qb, kb, proj, cols["vb"], swa_sinks[l], seq)
        kme, kmo = moba_gate_mats(kmean, batch, seq // MOBA_BLOCK)
        y_c = moba_attention(qm, km, proj, cols["vm"], kme, kmo, batch, seq)
        x2d, h_ffn = out_proj(x2d, y_a, y_b, y_c, proj, cols["ga"],
                              w_up_a[l].astype(BF16), w_up_b[l].astype(BF16), w_up_c[l].astype(BF16),
                              w_o[l].astype(BF16), norm_ffn[l])

        wr_t = jnp.concatenate([w_router_expert[l].T, w_router_group[l].T,
                                jnp.zeros((LANES - N_EXPERTS - N_GROUPS, d), F32)], axis=0).astype(BF16)
        route, counts = router(h_ffn, wr_t)
        dest, wts, blk_exp, nvalid, n_slots = _moe_plan(route, counts, t)
        xs = dispatch(h_ffn, dest, n_slots)
        ys = experts(xs, blk_exp, nvalid, w_exp_gate[l].astype(BF16), w_exp_up[l].astype(BF16),
                     w_exp_down[l].astype(BF16))
        next_gain = norm_mix[l + 1] if l + 1 < depth else jnp.ones((d,), F32)
        x2d, h = combine(x2d, ys, dest, wts, next_gain)
    return x2d.reshape(batch, seq, d)
```

```python
import functools

import jax
import jax.numpy as jnp
from jax import lax
from jax.experimental import pallas as pl
from jax.experimental.pallas import tpu as pltpu

F32 = jnp.float32
BF16 = jnp.bfloat16

HEAD_DIM = 64
ROPE_THETA = 10000.0
RMS_EPS = 1e-6
NEG_INF = -1e30

POOL_WINDOWS = (2, 4, 8, 16)
POOL_GC = 128

SWA_HQ = 16
SWA_HKV = 2
SWA_BLOCK = 128

MOBA_HEADS = 8
MOBA_BLOCK = 256
MOBA_TOPK = 3

N_GROUPS = 4
EXPERTS_PER_GROUP = 8
N_EXPERTS = 32
MOE_BLOCK = 256

LANES = 128
VMEM_LIMIT = 56 * 1024 * 1024

_NT = (((1,), (1,)), ((), ()))


def _params(n_axes, vmem=VMEM_LIMIT):
    return pltpu.CompilerParams(dimension_semantics=("arbitrary",) * n_axes, vmem_limit_bytes=vmem)


def _rmsnorm_kernel(x_ref, g_ref, o_ref):
    x = x_ref[...]
    ms = jnp.mean(x * x, axis=-1, keepdims=True)
    o_ref[...] = (x * lax.rsqrt(ms + RMS_EPS) * g_ref[...]).astype(o_ref.dtype)


def rmsnorm_cast(x2d, gain, tm=512):
    t, d = x2d.shape
    return pl.pallas_call(
        _rmsnorm_kernel,
        grid=(t // tm,),
        in_specs=[pl.BlockSpec((tm, d), lambda i: (i, 0)), pl.BlockSpec((1, d), lambda i: (0, 0))],
        out_specs=pl.BlockSpec((tm, d), lambda i: (i, 0)),
        out_shape=jax.ShapeDtypeStruct((t, d), BF16),
        compiler_params=_params(1),
        name="rmsnorm",
    )(x2d, gain.reshape(1, d))


IN_PROJ_TN = 256


def _in_proj_kernel(tile_map_ref, a_ref, w_ref, o_ref):
    del tile_map_ref
    o_ref[...] = jnp.dot(a_ref[...], w_ref[...].astype(BF16), preferred_element_type=F32).astype(o_ref.dtype)


def in_proj(h, w_in, layer, tile_map, tm=2048):
    t, k = h.shape
    n = w_in.shape[2]
    tn = IN_PROJ_TN
    tm = min(tm, t)
    grid_spec = pltpu.PrefetchScalarGridSpec(
        num_scalar_prefetch=1,
        grid=(t // tm, n // tn),
        in_specs=[pl.BlockSpec((tm, k), lambda i, j, tmap: (i, 0)),
                  pl.BlockSpec((None, k, tn), lambda i, j, tmap: (layer, 0, j))],
        out_specs=pl.BlockSpec((tm, tn), lambda i, j, tmap: (i, tmap[j])),
    )
    return pl.pallas_call(
        _in_proj_kernel,
        grid_spec=grid_spec,
        out_shape=jax.ShapeDtypeStruct((t, n), BF16),
        compiler_params=_params(2),
        name="in_proj",
    )(tile_map, h, w_in)


def _norm_rope(x, gain, cos, sin_signed, blockdiag, first_half):
    x2 = x * x
    hi = x2.astype(BF16)
    lo = (x2 - hi.astype(F32)).astype(BF16)
    ss = jnp.dot(hi, blockdiag, preferred_element_type=F32) + jnp.dot(lo, blockdiag, preferred_element_type=F32)
    y = x * lax.rsqrt(ss * (1.0 / HEAD_DIM) + RMS_EPS) * gain
    rot = jnp.where(first_half, pltpu.roll(y, LANES - HEAD_DIM // 2, 1), pltpu.roll(y, HEAD_DIM // 2, 1))
    return y * cos + rot * sin_signed


def _prep_kernel(qb_ref, kb_ref, qm_ref, km_ref, cos_ref, sin_ref, gains_ref,
                 qb_o, kb_o, qm_o, km_o, kmean_o, *, scale):
    tm = qb_ref.shape[0]
    cos = cos_ref[...]
    sin = sin_ref[...]
    lane = lax.broadcasted_iota(jnp.int32, (tm, LANES), 1)
    first_half = (lane % HEAD_DIM) < (HEAD_DIM // 2)
    r = lax.broadcasted_iota(jnp.int32, (LANES, LANES), 0) // HEAD_DIM
    c = lax.broadcasted_iota(jnp.int32, (LANES, LANES), 1) // HEAD_DIM
    blockdiag = (r == c).astype(BF16)

    def run(src, dst, gain_row, mult, n_groups):
        outs = []
        gain = gains_ref[gain_row:gain_row + 1, :]
        for g in range(n_groups):
            sl = slice(g * LANES, (g + 1) * LANES)
            y = _norm_rope(src[:, sl].astype(F32), gain, cos, sin, blockdiag, first_half)
            if mult != 1.0:
                y = y * mult
            dst[:, sl] = y.astype(dst.dtype)
            outs.append(y)
        return outs

    run(qb_ref, qb_o, 0, scale, qb_ref.shape[1] // LANES)
    run(kb_ref, kb_o, 1, 1.0, kb_ref.shape[1] // LANES)
    run(qm_ref, qm_o, 2, scale, qm_ref.shape[1] // LANES)
    kms = run(km_ref, km_o, 3, 1.0, km_ref.shape[1] // LANES)
    for g, y in enumerate(kms):
        for blk in range(tm // MOBA_BLOCK):
            s = jnp.sum(y[blk * MOBA_BLOCK:(blk + 1) * MOBA_BLOCK, :], axis=0, keepdims=True)
            kmean_o[blk, :, g * LANES:(g + 1) * LANES] = s * (1.0 / MOBA_BLOCK)


def qk_prep(proj, cols, cos_t, sin_t, gains, seq, tm=512):
    t = proj.shape[0]
    tiles_per_seq = seq // tm
    wqb, wkb, wqm, wkm = SWA_HQ * HEAD_DIM, SWA_HKV * HEAD_DIM, MOBA_HEADS * HEAD_DIM, MOBA_HEADS * HEAD_DIM

    def col_spec(width, off):
        return pl.BlockSpec((tm, width), lambda i, o=off // width: (i, o))

    tab_spec = pl.BlockSpec((tm, LANES), lambda i: (i % tiles_per_seq, 0))
    return pl.pallas_call(
        functools.partial(_prep_kernel, scale=HEAD_DIM ** -0.5),
        grid=(t // tm,),
        in_specs=[col_spec(wqb, cols["qb"]), col_spec(wkb, cols["kb"]), col_spec(wqm, cols["qm"]),
                  col_spec(wkm, cols["km"]), tab_spec, tab_spec, pl.BlockSpec((8, LANES), lambda i: (0, 0))],
        out_specs=[pl.BlockSpec((tm, wqb), lambda i: (i, 0)), pl.BlockSpec((tm, wkb), lambda i: (i, 0)),
                   pl.BlockSpec((tm, wqm), lambda i: (i, 0)), pl.BlockSpec((tm, wkm), lambda i: (i, 0)),
                   pl.BlockSpec((tm // MOBA_BLOCK, 1, wkm), lambda i: (i, 0, 0))],
        out_shape=[jax.ShapeDtypeStruct((t, wqb), BF16), jax.ShapeDtypeStruct((t, wkb), BF16),
                   jax.ShapeDtypeStruct((t, wqm), BF16), jax.ShapeDtypeStruct((t, wkm), BF16),
                   jax.ShapeDtypeStruct((t // MOBA_BLOCK, 1, wkm), F32)],
        compiler_params=_params(1),
        name="qk_prep",
    )(proj, proj, proj, proj, cos_t, sin_t, gains)


def _pool_kernel(cur_ref, prev_ref, pw_ref, ps_ref, o_ref, *, tiles_per_seq):
    tm = cur_ref.shape[0]
    i = pl.program_id(0)
    tile_in_seq = i % tiles_per_seq
    row = lax.broadcasted_iota(jnp.int32, (tm, tm), 0)
    col = lax.broadcasted_iota(jnp.int32, (tm, tm), 1)
    pos = tile_in_seq * tm + lax.broadcasted_iota(jnp.int32, (tm, 1), 0)
    has_prev = (tile_in_seq > 0).astype(F32)
    for g, w in enumerate(POOL_WINDOWS):
        sl = slice(g * POOL_GC, (g + 1) * POOL_GC)
        cur = cur_ref[:, sl]
        prev = prev_ref[:, sl]
        band_cur = ((row >= col) & (row - col < w)).astype(BF16)
        band_prev = (row + tm - col < w).astype(BF16)
        win = jnp.dot(band_cur, cur, preferred_element_type=F32)
        win = win + has_prev * jnp.dot(band_prev, prev, preferred_element_type=F32)
        cnt = jnp.minimum(pos + 1, w).astype(F32)
        pooled = (win / cnt - cur.astype(F32)).astype(BF16)
        y = jnp.dot(pooled, pw_ref[g], preferred_element_type=F32) * ps_ref[:, sl]
        o_ref[:, sl] = y.astype(o_ref.dtype)


def pool_mixer(proj, col_off, pool_w, pool_scale, seq, tm=256):
    t = proj.shape[0]
    width = len(POOL_WINDOWS) * POOL_GC
    cb = col_off // width
    tiles_per_seq = seq // tm
    return pl.pallas_call(
        functools.partial(_pool_kernel, tiles_per_seq=tiles_per_seq),
        grid=(t // tm,),
        in_specs=[pl.BlockSpec((tm, width), lambda i: (i, cb)),
                  pl.BlockSpec((tm, width), lambda i: (jnp.maximum(i - 1, 0), cb)),
                  pl.BlockSpec((len(POOL_WINDOWS), POOL_GC, POOL_GC), lambda i: (0, 0, 0)),
                  pl.BlockSpec((1, width), lambda i: (0, 0))],
        out_specs=pl.BlockSpec((tm, width), lambda i: (i, 0)),
        out_shape=jax.ShapeDtypeStruct((t, width), BF16),
        compiler_params=_params(1),
        name="pool_mixer",
    )(proj, proj, pool_w.astype(BF16), pool_scale.reshape(1, width))


def _half_select(x, lane, low):
    keep = (lane < HEAD_DIM) if low else (lane >= HEAD_DIM)
    return jnp.where(keep, x, 0.0)


def _swa_kernel(q_ref, kc_ref, kp_ref, vc_ref, vp_ref, sink_ref, o_ref, *, blocks_per_seq):
    blk = SWA_BLOCK
    i = pl.program_id(0)
    not_first = (i % blocks_per_seq) > 0
    lane = lax.broadcasted_iota(jnp.int32, (2 * blk, LANES), 1)

    def group_mats(cur_ref, prev_ref):
        x = jnp.concatenate([prev_ref[...], cur_ref[...]], axis=0).astype(F32)
        g0_lo = _half_select(x, lane, True)
        g1_hi = _half_select(x, lane, False)
        g0_hi = pltpu.roll(g0_lo, HEAD_DIM, 1)
        g1_lo = pltpu.roll(g1_hi, HEAD_DIM, 1)
        return ((g0_lo.astype(BF16), g0_hi.astype(BF16)), (g1_lo.astype(BF16), g1_hi.astype(BF16)))

    kmats = group_mats(kc_ref, kp_ref)
    vmats = group_mats(vc_ref, vp_ref)
    qi = lax.broadcasted_iota(jnp.int32, (blk, 2 * blk), 0)
    kj = lax.broadcasted_iota(jnp.int32, (blk, 2 * blk), 1)
    dist = qi + blk - kj
    band = (dist >= 0) & (dist < blk) & (not_first | (kj >= blk))
    rep = SWA_HQ // SWA_HKV
    for p in range(SWA_HQ // 2):
        grp = (2 * p) // rep
        qp = q_ref[:, p * LANES:(p + 1) * LANES]
        out = jnp.zeros((blk, LANES), F32)
        for par in range(2):
            s = lax.dot_general(qp, kmats[grp][par], _NT, preferred_element_type=F32)
            s = jnp.where(band, s, NEG_INF)
            sink = sink_ref[0, 2 * p + par]
            m = jnp.maximum(jnp.max(s, axis=1, keepdims=True), sink)
            e = jnp.exp(s - m)
            den = jnp.sum(e, axis=1, keepdims=True) + jnp.exp(sink - m)
            pr = (e / den).astype(BF16)
            out = out + jnp.dot(pr, vmats[grp][par], preferred_element_type=F32)
        o_ref[:, p * LANES:(p + 1) * LANES] = out.astype(o_ref.dtype)


def swa_attention(q, k, proj, v_col, sinks, seq):
    t = q.shape[0]
    blk = SWA_BLOCK
    bps = seq // blk
    kvw = SWA_HKV * HEAD_DIM
    vb = v_col // kvw

    def prev(i):
        return jnp.maximum(i - 1, 0)

    return pl.pallas_call(
        functools.partial(_swa_kernel, blocks_per_seq=bps),
        grid=(t // blk,),
        in_specs=[pl.BlockSpec((blk, q.shape[1]), lambda i: (i, 0)),
                  pl.BlockSpec((blk, kvw), lambda i: (i, 0)),
                  pl.BlockSpec((blk, kvw), lambda i: (prev(i), 0)),
                  pl.BlockSpec((blk, kvw), lambda i: (i, vb)),
                  pl.BlockSpec((blk, kvw), lambda i: (prev(i), vb)),
                  pl.BlockSpec((1, SWA_HQ), lambda i: (0, 0), memory_space=pltpu.SMEM)],
        out_specs=pl.BlockSpec((blk, q.shape[1]), lambda i: (i, 0)),
        out_shape=jax.ShapeDtypeStruct(q.shape, BF16),
        compiler_params=_params(1),
        name="swa_attention",
    )(q, k, k, proj, proj, sinks.reshape(1, SWA_HQ).astype(F32))


VT_ROWS = HEAD_DIM + 16


def _moba_kernel(q_ref, k_ref, v_ref, km_ref, o_ref, vt_ref, ke_ref, ko_ref, m_ref, acc_ref,
                 *, nblk, qt):
    blk = MOBA_BLOCK
    sub = qt // blk
    npair = q_ref.shape[1] // LANES
    qi = pl.program_id(2)
    lane = lax.broadcasted_iota(jnp.int32, (blk, LANES), 1)
    low = lane < HEAD_DIM

    @pl.when(qi == 0)
    def _():
        ones_row = (lax.broadcasted_iota(jnp.int32, (VT_ROWS - HEAD_DIM, vt_ref.shape[2]), 0) == 0).astype(BF16)
        for h in range(2 * npair):
            vt_ref[h, HEAD_DIM:VT_ROWS, :] = ones_row

        def fill(j, c):
            off = pl.multiple_of(j * blk, blk)
            vt = v_ref[pl.ds(off, blk), :].astype(F32).T.astype(BF16)
            for h in range(2 * npair):
                vt_ref[h, 0:HEAD_DIM, pl.ds(off, blk)] = vt[h * HEAD_DIM:(h + 1) * HEAD_DIM, :]
            for p in range(npair):
                kf = k_ref[pl.ds(off, blk), p * LANES:(p + 1) * LANES].astype(F32)
                ke_ref[p, pl.ds(off, blk), :] = jnp.where(low, kf, (lane - HEAD_DIM == j).astype(F32)).astype(BF16)
                ko_ref[p, pl.ds(off, blk), :] = jnp.where(low, (lane == j).astype(F32), kf).astype(BF16)
            return c
        lax.fori_loop(0, nblk, fill, 0)

    block_of_row = lax.broadcasted_iota(jnp.int32, (HEAD_DIM, blk), 0)

    def block_bias(gates_t, own):
        g = jnp.where(block_of_row < own, gates_t, -jnp.inf)
        sel = block_of_row == own
        for _ in range(MOBA_TOPK):
            mx = jnp.max(g, axis=0, keepdims=True)
            idx = jnp.min(jnp.where(g == mx, block_of_row, 1 << 20), axis=0, keepdims=True)
            hit = (block_of_row == idx) & (mx > -jnp.inf)
            sel = sel | hit
            g = jnp.where(hit, -jnp.inf, g)
        return jnp.where(sel, 0.0, NEG_INF)

    q_ext = []
    for p in range(npair):
        ext_e, ext_o = [], []
        for c in range(sub):
            own = qi * sub + c
            qp = q_ref[c * blk:(c + 1) * blk, p * LANES:(p + 1) * LANES]
            gates_t = lax.dot_general(km_ref[p], qp, _NT, preferred_element_type=F32)
            bias_t = jnp.concatenate([block_bias(gates_t[0:HEAD_DIM], own), block_bias(gates_t[HEAD_DIM:], own)],
                                     axis=0)
            bias = bias_t.T
            qf = qp.astype(F32)
            ext_e.append(jnp.where(low, qf, bias).astype(BF16))
            ext_o.append(jnp.where(low, bias, qf).astype(BF16))
        q_ext.append((jnp.concatenate(ext_e, axis=0), jnp.concatenate(ext_o, axis=0)))

    m_ref[...] = jnp.full(m_ref.shape, -jnp.inf, F32)
    acc_ref[...] = jnp.zeros(acc_ref.shape, F32)
    rows = sub * blk
    key_idx = lax.broadcasted_iota(jnp.int32, (rows, qt), 0)
    qry_idx = lax.broadcasted_iota(jnp.int32, (rows, qt), 1)

    def steps(jts, causal):
        tasks = [(pl.multiple_of(jt * rows, rows), h) for jt in jts for h in range(2 * npair)]

        def scores(off, h):
            k_ext = (ke_ref if h % 2 == 0 else ko_ref)[h // 2, pl.ds(off, rows), :]
            return lax.dot_general(k_ext, q_ext[h // 2][h % 2], _NT, preferred_element_type=F32)

        st_next = scores(*tasks[0])
        for n, (off, h) in enumerate(tasks):
            st = st_next
            if n + 1 < len(tasks):
                st_next = scores(*tasks[n + 1])
            if causal:
                st = jnp.where(key_idx <= qry_idx, st, NEG_INF)
            sb = st.astype(BF16)
            m = m_ref[h]
            m_new = jnp.maximum(m, jnp.max(sb, axis=0, keepdims=True).astype(F32))
            alpha = jnp.exp(m - m_new)
            pm = jnp.exp(sb - m_new.astype(BF16))
            pv = jnp.dot(vt_ref[h, :, pl.ds(off, rows)], pm, preferred_element_type=F32)
            acc_ref[h] = acc_ref[h] * alpha + pv
            m_ref[h] = m_new

    def body(u, c):
        steps([2 * u, 2 * u + 1], False)
        return c

    lax.fori_loop(0, qi // 2, body, 0)

    @pl.when(qi % 2 == 1)
    def _():
        steps([qi - 1], False)

    steps([qi], True)
    outs = [acc_ref[h, 0:HEAD_DIM, :] / acc_ref[h, HEAD_DIM:HEAD_DIM + 1, :] for h in range(2 * npair)]
    o_ref[...] = jnp.concatenate(outs, axis=0).T.astype(o_ref.dtype)


def moba_attention(q, k, proj, v_col, gate_mats, batch, seq, qt=512, npair=2):
    t, width = q.shape
    nblk = seq // MOBA_BLOCK
    qt = min(qt, seq)
    gw = npair * LANES
    q3 = q.reshape(batch, seq, width)
    k3 = k.reshape(batch, seq, width)
    p3 = proj.reshape(batch, seq, proj.shape[1])
    vb = v_col // gw

    def resident(shape, index_map):
        return pl.BlockSpec(shape, index_map, pipeline_mode=pl.Buffered(1))

    out = pl.pallas_call(
        functools.partial(_moba_kernel, nblk=nblk, qt=qt),
        grid=(batch, width // gw, seq // qt),
        in_specs=[pl.BlockSpec((None, qt, gw), lambda b, g, i: (b, i, g)),
                  resident((None, seq, gw), lambda b, g, i: (b, 0, g)),
                  resident((None, seq, gw), lambda b, g, i: (b, 0, vb + g)),
                  pl.BlockSpec((None, npair, LANES, LANES), lambda b, g, i: (b, g, 0, 0))],
        out_specs=pl.BlockSpec((None, qt, gw), lambda b, g, i: (b, i, g)),
        out_shape=jax.ShapeDtypeStruct((batch, seq, width), BF16),
        scratch_shapes=[pltpu.VMEM((2 * npair, VT_ROWS, seq), BF16),
                        pltpu.VMEM((npair, seq, LANES), BF16), pltpu.VMEM((npair, seq, LANES), BF16),
                        pltpu.VMEM((2 * npair, 1, qt), F32),
                        pltpu.VMEM((2 * npair, VT_ROWS, qt), F32)],
        compiler_params=_params(3),
        name="moba_attention",
    )(q3, k3, p3, gate_mats)
    return out.reshape(t, width)


def moba_gate_mats(kmean, batch, nblk):
    km = kmean.reshape(batch, nblk, MOBA_HEADS // 2, 2, HEAD_DIM).astype(BF16)
    km = jnp.transpose(km, (0, 2, 3, 1, 4))
    pad_rows = LANES // 2 - nblk
    even = jnp.pad(km[:, :, 0], ((0, 0), (0, 0), (LANES // 2, pad_rows), (0, HEAD_DIM)))
    odd = jnp.pad(km[:, :, 1], ((0, 0), (0, 0), (0, LANES // 2 + pad_rows), (HEAD_DIM, 0)))
    return even + odd


def _outproj_kernel(x_ref, ya_ref, yb_ref, yc_ref, ga_ref, gb_ref, gc_ref,
                    wa_ref, wb_ref, wc_ref, wo_ref, gain_ref, xo_ref, h_ref):
    def gate(g_ref):
        return jax.nn.sigmoid(g_ref[...].astype(F32))

    merged = gate(ga_ref) * jnp.dot(ya_ref[...], wa_ref[...], preferred_element_type=F32)
    merged = merged + gate(gb_ref) * jnp.dot(yb_ref[...], wb_ref[...], preferred_element_type=F32)
    merged = merged + gate(gc_ref) * jnp.dot(yc_ref[...], wc_ref[...], preferred_element_type=F32)
    xn = x_ref[...] + jnp.dot(merged.astype(BF16), wo_ref[...], preferred_element_type=F32)
    xo_ref[...] = xn
    ms = jnp.mean(xn * xn, axis=-1, keepdims=True)
    h_ref[...] = xn * lax.rsqrt(ms + RMS_EPS) * gain_ref[...]


def out_proj(x2d, ya, yb, yc, proj, gate_col, wa, wb, wc, wo, gain, tm=256):
    t, d = x2d.shape
    gb0 = gate_col // d

    def rows(width):
        return pl.BlockSpec((tm, width), lambda i: (i, 0))

    def whole(w):
        return pl.BlockSpec(w.shape, lambda i: (0, 0), pipeline_mode=pl.Buffered(1))

    return pl.pallas_call(
        _outproj_kernel,
        grid=(t // tm,),
        in_specs=[rows(d), rows(ya.shape[1]), rows(yb.shape[1]), rows(yc.shape[1]),
                  pl.BlockSpec((tm, d), lambda i: (i, gb0)),
                  pl.BlockSpec((tm, d), lambda i: (i, gb0 + 1)),
                  pl.BlockSpec((tm, d), lambda i: (i, gb0 + 2)),
                  whole(wa), whole(wb), whole(wc), whole(wo),
                  pl.BlockSpec((1, d), lambda i: (0, 0))],
        out_specs=[rows(d), rows(d)],
        out_shape=[jax.ShapeDtypeStruct((t, d), F32), jax.ShapeDtypeStruct((t, d), F32)],
        compiler_params=_params(1),
        name="out_proj",
    )(x2d, ya, yb, yc, proj, proj, proj, wa, wb, wc, wo, gain.reshape(1, d))


def _router_kernel(h_ref, wr_ref, route_ref, count_ref, run_ref):
    tm = h_ref.shape[0]
    epg = EXPERTS_PER_GROUP

    @pl.when(pl.program_id(0) == 0)
    def _():
        run_ref[...] = jnp.zeros(run_ref.shape, F32)

    logits = lax.dot_general(wr_ref[...], h_ref[...].astype(BF16), _NT, preferred_element_type=F32)
    row8 = lax.broadcasted_iota(jnp.int32, (epg, tm), 0)
    glog = jnp.where(row8 < N_GROUPS, logits[N_EXPERTS:N_EXPERTS + epg, :], -jnp.inf)
    gmax = jnp.max(glog, axis=0, keepdims=True)
    gsel = jnp.min(jnp.where(glog == gmax, row8, epg), axis=0, keepdims=True)
    p_grp = 1.0 / jnp.sum(jnp.exp(glog - gmax), axis=0, keepdims=True)
    within = jnp.zeros((epg, tm), F32)
    for g in range(N_GROUPS):
        within = jnp.where(gsel == g, logits[g * epg:(g + 1) * epg, :], within)
    wmax = jnp.max(within, axis=0, keepdims=True)
    e = jnp.exp(within - wmax)
    prob = e / jnp.sum(e, axis=0, keepdims=True)
    p1 = jnp.max(prob, axis=0, keepdims=True)
    i1 = jnp.min(jnp.where(prob == p1, row8, epg), axis=0, keepdims=True)
    rest = jnp.where(row8 == i1, -jnp.inf, prob)
    p2 = jnp.max(rest, axis=0, keepdims=True)
    i2 = jnp.min(jnp.where(rest == p2, row8, epg), axis=0, keepdims=True)
    norm = p1 + p2
    w1 = p_grp * p1 / norm
    w2 = p_grp * p2 / norm
    e1 = gsel * epg + i1
    e2 = gsel * epg + i2

    row32 = lax.broadcasted_iota(jnp.int32, (N_EXPERTS, tm), 0)
    oh1 = (row32 == e1).astype(F32)
    oh2 = (row32 == e2).astype(F32)
    oh = oh1 + oh2
    before = (lax.broadcasted_iota(jnp.int32, (tm, tm), 0) < lax.broadcasted_iota(jnp.int32, (tm, tm), 1)).astype(BF16)
    prior = jnp.dot(oh.astype(BF16), before, preferred_element_type=F32) + run_ref[:, 0:1]
    r1 = jnp.sum(oh1 * prior, axis=0, keepdims=True)
    r2 = jnp.sum(oh2 * prior, axis=0, keepdims=True)
    run_new = run_ref[...] + jnp.sum(oh, axis=1, keepdims=True)
    run_ref[...] = run_new
    count_ref[...] = run_new
    zero = jnp.zeros((1, tm), F32)
    route_ref[...] = jnp.concatenate([e1.astype(F32), e2.astype(F32), r1, r2, w1, w2, zero, zero], axis=0)


def router(h, wr_t, tm=512):
    t, d = h.shape
    return pl.pallas_call(
        _router_kernel,
        grid=(t // tm,),
        in_specs=[pl.BlockSpec((tm, d), lambda i: (i, 0)), pl.BlockSpec(wr_t.shape, lambda i: (0, 0))],
        out_specs=[pl.BlockSpec((8, tm), lambda i: (0, i)), pl.BlockSpec((N_EXPERTS, LANES), lambda i: (0, 0))],
        out_shape=[jax.ShapeDtypeStruct((8, t), F32), jax.ShapeDtypeStruct((N_EXPERTS, LANES), F32)],
        scratch_shapes=[pltpu.VMEM((N_EXPERTS, LANES), F32)],
        compiler_params=_params(1),
        name="router",
    )(h, wr_t)


def _row_copy(src_ref, src_row, dst_ref, dst_row, sem):
    return pltpu.make_async_copy(src_ref.at[pl.ds(src_row, 1)], dst_ref.at[pl.ds(dst_row, 1)], sem)


def _dispatch_kernel(dest_ref, h_ref, init_ref, xs_ref, sem):
    del init_ref
    tm = h_ref.shape[0]

    def start(r, c):
        for k in range(2):
            _row_copy(h_ref, r, xs_ref, dest_ref[k, r], sem).start()
        return c

    def wait(r, c):
        for k in range(2):
            _row_copy(h_ref, r, xs_ref, dest_ref[k, r], sem).wait()
        return c

    lax.fori_loop(0, tm, start, 0, unroll=8)
    lax.fori_loop(0, tm, wait, 0, unroll=8)


def dispatch(h, dest, n_slots, tm=256):
    t, d = h.shape
    dest3 = dest.reshape(2, t // tm, tm).transpose(1, 0, 2)
    return pl.pallas_call(
        _dispatch_kernel,
        grid=(t // tm,),
        in_specs=[pl.BlockSpec((None, 2, tm), lambda i: (i, 0, 0), memory_space=pltpu.SMEM),
                  pl.BlockSpec((tm, d), lambda i: (i, 0)),
                  pl.BlockSpec(memory_space=pl.ANY)],
        out_specs=pl.BlockSpec(memory_space=pl.ANY),
        out_shape=jax.ShapeDtypeStruct((n_slots, d), h.dtype),
        input_output_aliases={2: 0},
        scratch_shapes=[pltpu.SemaphoreType.DMA],
        compiler_params=_params(1),
        name="moe_dispatch",
    )(dest3, h, jnp.zeros((n_slots, d), h.dtype))


def _expert_kernel(bexp_ref, nvalid_ref, xs_ref, wg_ref, wu_ref, wd_ref, ys_ref):
    n = pl.program_id(0)
    nv = nvalid_ref[n]

    @pl.when(nv > 0)
    def _():
        row = lax.broadcasted_iota(jnp.int32, xs_ref.shape, 0)
        x = jnp.where(row < nv, xs_ref[...], 0.0).astype(BF16)
        g = jnp.dot(x, wg_ref[...], preferred_element_type=F32)
        u = jnp.dot(x, wu_ref[...], preferred_element_type=F32)
        act = (jax.nn.silu(g) * u).astype(BF16)
        ys_ref[...] = jnp.dot(act, wd_ref[...], preferred_element_type=F32)

    @pl.when(nv <= 0)
    def _():
        ys_ref[...] = jnp.zeros(ys_ref.shape, ys_ref.dtype)


def experts(xs, blk_exp, nvalid, wg, wu, wd):
    n_slots, d = xs.shape
    ff = wg.shape[2]
    nb = n_slots // MOE_BLOCK
    grid_spec = pltpu.PrefetchScalarGridSpec(
        num_scalar_prefetch=2,
        grid=(nb,),
        in_specs=[pl.BlockSpec((MOE_BLOCK, d), lambda n, be, nv: (n, 0)),
                  pl.BlockSpec((None, d, ff), lambda n, be, nv: (be[n], 0, 0)),
                  pl.BlockSpec((None, d, ff), lambda n, be, nv: (be[n], 0, 0)),
                  pl.BlockSpec((None, ff, d), lambda n, be, nv: (be[n], 0, 0))],
        out_specs=pl.BlockSpec((MOE_BLOCK, d), lambda n, be, nv: (n, 0)),
    )
    return pl.pallas_call(
        _expert_kernel,
        grid_spec=grid_spec,
        out_shape=jax.ShapeDtypeStruct((n_slots, d), F32),
        compiler_params=_params(1),
        name="moe_experts",
    )(blk_exp, nvalid, xs, wg, wu, wd)


def _combine_kernel(dest_ref, x_ref, w_ref, gain_ref, ys_ref, xo_ref, h_ref, buf_ref, sem):
    tm = x_ref.shape[0]

    def start(r, c):
        for k in range(2):
            _row_copy(ys_ref, dest_ref[k, r], buf_ref.at[k], r, sem).start()
        return c

    def wait(r, c):
        for k in range(2):
            _row_copy(ys_ref, dest_ref[k, r], buf_ref.at[k], r, sem).wait()
        return c

    lax.fori_loop(0, tm, start, 0, unroll=8)
    lax.fori_loop(0, tm, wait, 0, unroll=8)
    w = w_ref[...]
    xn = x_ref[...] + buf_ref[0] * w[:, 0:1] + buf_ref[1] * w[:, 1:2]
    xo_ref[...] = xn
    ms = jnp.mean(xn * xn, axis=-1, keepdims=True)
    h_ref[...] = (xn * lax.rsqrt(ms + RMS_EPS) * gain_ref[...]).astype(h_ref.dtype)


def combine(x2d, ys, dest, wts, gain, tm=256):
    t, d = x2d.shape
    dest3 = dest.reshape(2, t // tm, tm).transpose(1, 0, 2)
    return pl.pallas_call(
        _combine_kernel,
        grid=(t // tm,),
        in_specs=[pl.BlockSpec((None, 2, tm), lambda i: (i, 0, 0), memory_space=pltpu.SMEM),
                  pl.BlockSpec((tm, d), lambda i: (i, 0)),
                  pl.BlockSpec((tm, 2), lambda i: (i, 0)),
                  pl.BlockSpec((1, d), lambda i: (0, 0)),
                  pl.BlockSpec(memory_space=pl.ANY)],
        out_specs=[pl.BlockSpec((tm, d), lambda i: (i, 0)), pl.BlockSpec((tm, d), lambda i: (i, 0))],
        out_shape=[jax.ShapeDtypeStruct((t, d), F32), jax.ShapeDtypeStruct((t, d), BF16)],
        scratch_shapes=[pltpu.VMEM((2, tm, d), F32), pltpu.SemaphoreType.DMA],
        compiler_params=_params(1),
        name="moe_combine",
    )(dest3, x2d, wts, gain.reshape(1, d), ys)


def _rope_tables(seq):
    inv_freq = 1.0 / (ROPE_THETA ** (jnp.arange(0, HEAD_DIM, 2, dtype=F32) / HEAD_DIM))
    ang = jnp.arange(seq, dtype=F32)[:, None] * inv_freq[None, :]
    cos, sin = jnp.cos(ang), jnp.sin(ang)
    cos_t = jnp.concatenate([cos, cos, cos, cos], axis=1)
    sin_t = jnp.concatenate([-sin, sin, -sin, sin], axis=1)
    return cos_t, sin_t


def _moe_plan(route, counts, n_tokens):
    eid = route[0:2].astype(jnp.int32)
    rank = route[2:4].astype(jnp.int32)
    wts = route[4:6].T
    cnt = counts[:, 0].astype(jnp.int32)
    padded = ((cnt + MOE_BLOCK - 1) // MOE_BLOCK) * MOE_BLOCK
    p_ends = jnp.cumsum(padded)
    p_starts = p_ends - padded
    hit = eid[..., None] == jnp.arange(N_EXPERTS, dtype=jnp.int32)
    dest = rank + jnp.sum(jnp.where(hit, p_starts, 0), axis=-1)
    n_blocks = -(-(n_tokens * 2) // MOE_BLOCK) + N_EXPERTS
    blk_start = jnp.arange(n_blocks, dtype=jnp.int32) * MOE_BLOCK
    blk_exp = jnp.minimum(jnp.searchsorted(p_ends, blk_start, side="right"), N_EXPERTS - 1).astype(jnp.int32)
    nvalid = jnp.clip(cnt[blk_exp] - (blk_start - p_starts[blk_exp]), 0, MOE_BLOCK).astype(jnp.int32)
    return dest, wts, blk_exp, nvalid, n_blocks * MOE_BLOCK


def kernel(x, norm_mix, w_in, pool_w, pool_scale, swa_q_norm, swa_k_norm, swa_sinks, moba_q_norm, moba_k_norm,
           w_up_a, w_up_b, w_up_c, w_o, norm_ffn, w_router_group, w_router_expert, w_exp_gate, w_exp_up, w_exp_down):
    batch, seq, d = x.shape
    t = batch * seq
    depth = w_in.shape[0]
    pool_width = len(POOL_WINDOWS) * POOL_GC
    wqb, wkv, wm = SWA_HQ * HEAD_DIM, SWA_HKV * HEAD_DIM, MOBA_HEADS * HEAD_DIM

    names = ["xa", "qb", "kb", "vb", "qm", "km", "vm", "ga", "gb", "gc"]
    widths = [pool_width, wqb, wkv, wkv, wm, wm, wm, d, d, d]
    src, o = {}, 0
    for nme, w in zip(names, widths):
        src[nme] = (o, w)
        o += w
    order = ["ga", "gb", "gc", "qb", "qm", "km", "vm", "xa", "kb", "vb"]
    cols, o = {}, 0
    for nme in order:
        cols[nme] = o
        o += src[nme][1]
    tile_map = [0] * (o // IN_PROJ_TN)
    for nme in order:
        if nme == "vb":
            continue
        width = src[nme][1] + (src["vb"][1] if nme == "kb" else 0)
        for k in range(width // IN_PROJ_TN):
            tile_map[src[nme][0] // IN_PROJ_TN + k] = cols[nme] // IN_PROJ_TN + k
    tile_map = jnp.asarray(tile_map, jnp.int32)

    cos_t, sin_t = _rope_tables(seq)
    x2d = x.reshape(t, d)
    h = rmsnorm_cast(x2d, norm_mix[0])

    for l in range(depth):
        proj = in_proj(h, w_in, l, tile_map)

        def tile2(g):
            return jnp.concatenate([g, g], axis=0)

        gains = jnp.stack([tile2(swa_q_norm[l]), tile2(swa_k_norm[l]), tile2(moba_q_norm[l]), tile2(moba_k_norm[l])]
                          + [jnp.zeros((LANES,), F32)] * 4)
        qb, kb, qm, km, kmean = qk_prep(proj, cols, cos_t, sin_t, gains, seq)
        y_a = pool_mixer(proj, cols["xa"], pool_w[l], pool_scale[l], seq)
        y_b = swa_attention(qb, kb, proj, cols["vb"], swa_sinks[l], seq)
        gate_mats = moba_gate_mats(kmean, batch, seq // MOBA_BLOCK)
        y_c = moba_attention(qm, km, proj, cols["vm"], gate_mats, batch, seq)
        x2d, h_ffn = out_proj(x2d, y_a, y_b, y_c, proj, cols["ga"],
                              w_up_a[l].astype(BF16), w_up_b[l].astype(BF16), w_up_c[l].astype(BF16),
                              w_o[l].astype(BF16), norm_ffn[l])

        wr_t = jnp.concatenate([w_router_expert[l].T, w_router_group[l].T,
                                jnp.zeros((LANES - N_EXPERTS - N_GROUPS, d), F32)], axis=0).astype(BF16)
        route, counts = router(h_ffn, wr_t)
        dest, wts, blk_exp, nvalid, n_slots = _moe_plan(route, counts, t)
        xs = dispatch(h_ffn, dest, n_slots)
        ys = experts(xs, blk_exp, nvalid, w_exp_gate[l].astype(BF16), w_exp_up[l].astype(BF16),
                     w_exp_down[l].astype(BF16))
        next_gain = norm_mix[l + 1] if l + 1 < depth else jnp.ones((d,), F32)
        x2d, h = combine(x2d, ys, dest, wts, next_gain)
    return x2d.reshape(batch, seq, d)
```

```python
import functools

import jax
import jax.numpy as jnp
from jax import lax
from jax.experimental import pallas as pl
from jax.experimental.pallas import tpu as pltpu

F32 = jnp.float32
BF16 = jnp.bfloat16

HEAD_DIM = 64
ROPE_THETA = 10000.0
RMS_EPS = 1e-6
NEG_INF = -1e30

POOL_WINDOWS = (2, 4, 8, 16)
POOL_GC = 128

SWA_HQ = 16
SWA_HKV = 2
SWA_BLOCK = 128

MOBA_HEADS = 8
MOBA_BLOCK = 256
MOBA_TOPK = 3

N_GROUPS = 4
EXPERTS_PER_GROUP = 8
N_EXPERTS = 32
MOE_BLOCK = 256

LANES = 128
VMEM_LIMIT = 56 * 1024 * 1024

_NT = (((1,), (1,)), ((), ()))


def _params(n_axes, vmem=VMEM_LIMIT):
    return pltpu.CompilerParams(dimension_semantics=("arbitrary",) * n_axes, vmem_limit_bytes=vmem)


def _rmsnorm_kernel(x_ref, g_ref, o_ref):
    x = x_ref[...]
    ms = jnp.mean(x * x, axis=-1, keepdims=True)
    o_ref[...] = (x * lax.rsqrt(ms + RMS_EPS) * g_ref[...]).astype(o_ref.dtype)


def rmsnorm_cast(x2d, gain, tm=512):
    t, d = x2d.shape
    return pl.pallas_call(
        _rmsnorm_kernel,
        grid=(t // tm,),
        in_specs=[pl.BlockSpec((tm, d), lambda i: (i, 0)), pl.BlockSpec((1, d), lambda i: (0, 0))],
        out_specs=pl.BlockSpec((tm, d), lambda i: (i, 0)),
        out_shape=jax.ShapeDtypeStruct((t, d), BF16),
        compiler_params=_params(1),
        name="rmsnorm",
    )(x2d, gain.reshape(1, d))


IN_PROJ_TN = 256


def _in_proj_kernel(tile_map_ref, a_ref, w_ref, o_ref):
    del tile_map_ref
    o_ref[...] = jnp.dot(a_ref[...], w_ref[...].astype(BF16), preferred_element_type=F32).astype(o_ref.dtype)


def in_proj(h, w_in, layer, tile_map, tm=2048):
    t, k = h.shape
    n = w_in.shape[2]
    tn = IN_PROJ_TN
    tm = min(tm, t)
    grid_spec = pltpu.PrefetchScalarGridSpec(
        num_scalar_prefetch=1,
        grid=(t // tm, n // tn),
        in_specs=[pl.BlockSpec((tm, k), lambda i, j, tmap: (i, 0)),
                  pl.BlockSpec((None, k, tn), lambda i, j, tmap: (layer, 0, j))],
        out_specs=pl.BlockSpec((tm, tn), lambda i, j, tmap: (i, tmap[j])),
    )
    return pl.pallas_call(
        _in_proj_kernel,
        grid_spec=grid_spec,
        out_shape=jax.ShapeDtypeStruct((t, n), BF16),
        compiler_params=_params(2),
        name="in_proj",
    )(tile_map, h, w_in)


def _norm_rope(x, gain, cos, sin_signed, blockdiag, first_half):
    x2 = x * x
    hi = x2.astype(BF16)
    lo = (x2 - hi.astype(F32)).astype(BF16)
    ss = jnp.dot(hi, blockdiag, preferred_element_type=F32) + jnp.dot(lo, blockdiag, preferred_element_type=F32)
    y = x * lax.rsqrt(ss * (1.0 / HEAD_DIM) + RMS_EPS) * gain
    rot = jnp.where(first_half, pltpu.roll(y, LANES - HEAD_DIM // 2, 1), pltpu.roll(y, HEAD_DIM // 2, 1))
    return y * cos + rot * sin_signed


def _prep_kernel(qb_ref, kb_ref, qm_ref, km_ref, cos_ref, sin_ref, gains_ref,
                 qb_o, kb_o, qm_o, km_o, kmean_o, *, scale):
    tm = qb_ref.shape[0]
    cos = cos_ref[...]
    sin = sin_ref[...]
    lane = lax.broadcasted_iota(jnp.int32, (tm, LANES), 1)
    first_half = (lane % HEAD_DIM) < (HEAD_DIM // 2)
    r = lax.broadcasted_iota(jnp.int32, (LANES, LANES), 0) // HEAD_DIM
    c = lax.broadcasted_iota(jnp.int32, (LANES, LANES), 1) // HEAD_DIM
    blockdiag = (r == c).astype(BF16)

    def run(src, dst, gain_row, mult, n_groups):
        outs = []
        gain = gains_ref[gain_row:gain_row + 1, :]
        for g in range(n_groups):
            sl = slice(g * LANES, (g + 1) * LANES)
            y = _norm_rope(src[:, sl].astype(F32), gain, cos, sin, blockdiag, first_half)
            if mult != 1.0:
                y = y * mult
            dst[:, sl] = y.astype(dst.dtype)
            outs.append(y)
        return outs

    run(qb_ref, qb_o, 0, scale, qb_ref.shape[1] // LANES)
    run(kb_ref, kb_o, 1, 1.0, kb_ref.shape[1] // LANES)
    run(qm_ref, qm_o, 2, scale, qm_ref.shape[1] // LANES)
    kms = run(km_ref, km_o, 3, 1.0, km_ref.shape[1] // LANES)
    for g, y in enumerate(kms):
        for blk in range(tm // MOBA_BLOCK):
            s = jnp.sum(y[blk * MOBA_BLOCK:(blk + 1) * MOBA_BLOCK, :], axis=0, keepdims=True)
            kmean_o[blk, :, g * LANES:(g + 1) * LANES] = s * (1.0 / MOBA_BLOCK)


def qk_prep(proj, cols, cos_t, sin_t, gains, seq, tm=512):
    t = proj.shape[0]
    tiles_per_seq = seq // tm
    wqb, wkb, wqm, wkm = SWA_HQ * HEAD_DIM, SWA_HKV * HEAD_DIM, MOBA_HEADS * HEAD_DIM, MOBA_HEADS * HEAD_DIM

    def col_spec(width, off):
        return pl.BlockSpec((tm, width), lambda i, o=off // width: (i, o))

    tab_spec = pl.BlockSpec((tm, LANES), lambda i: (i % tiles_per_seq, 0))
    return pl.pallas_call(
        functools.partial(_prep_kernel, scale=HEAD_DIM ** -0.5),
        grid=(t // tm,),
        in_specs=[col_spec(wqb, cols["qb"]), col_spec(wkb, cols["kb"]), col_spec(wqm, cols["qm"]),
                  col_spec(wkm, cols["km"]), tab_spec, tab_spec, pl.BlockSpec((8, LANES), lambda i: (0, 0))],
        out_specs=[pl.BlockSpec((tm, wqb), lambda i: (i, 0)), pl.BlockSpec((tm, wkb), lambda i: (i, 0)),
                   pl.BlockSpec((tm, wqm), lambda i: (i, 0)), pl.BlockSpec((tm, wkm), lambda i: (i, 0)),
                   pl.BlockSpec((tm // MOBA_BLOCK, 1, wkm), lambda i: (i, 0, 0))],
        out_shape=[jax.ShapeDtypeStruct((t, wqb), BF16), jax.ShapeDtypeStruct((t, wkb), BF16),
                   jax.ShapeDtypeStruct((t, wqm), BF16), jax.ShapeDtypeStruct((t, wkm), BF16),
                   jax.ShapeDtypeStruct((t // MOBA_BLOCK, 1, wkm), F32)],
        compiler_params=_params(1),
        name="qk_prep",
    )(proj, proj, proj, proj, cos_t, sin_t, gains)


def _pool_kernel(cur_ref, prev_ref, pw_ref, ps_ref, o_ref, *, tiles_per_seq):
    tm = cur_ref.shape[0]
    i = pl.program_id(0)
    tile_in_seq = i % tiles_per_seq
    row = lax.broadcasted_iota(jnp.int32, (tm, tm), 0)
    col = lax.broadcasted_iota(jnp.int32, (tm, tm), 1)
    pos = tile_in_seq * tm + lax.broadcasted_iota(jnp.int32, (tm, 1), 0)
    has_prev = (tile_in_seq > 0).astype(F32)
    for g, w in enumerate(POOL_WINDOWS):
        sl = slice(g * POOL_GC, (g + 1) * POOL_GC)
        cur = cur_ref[:, sl]
        prev = prev_ref[:, sl]
        band_cur = ((row >= col) & (row - col < w)).astype(BF16)
        band_prev = (row + tm - col < w).astype(BF16)
        win = jnp.dot(band_cur, cur, preferred_element_type=F32)
        win = win + has_prev * jnp.dot(band_prev, prev, preferred_element_type=F32)
        cnt = jnp.minimum(pos + 1, w).astype(F32)
        pooled = (win / cnt - cur.astype(F32)).astype(BF16)
        y = jnp.dot(pooled, pw_ref[g], preferred_element_type=F32) * ps_ref[:, sl]
        o_ref[:, sl] = y.astype(o_ref.dtype)


def pool_mixer(proj, col_off, pool_w, pool_scale, seq, tm=256):
    t = proj.shape[0]
    width = len(POOL_WINDOWS) * POOL_GC
    cb = col_off // width
    tiles_per_seq = seq // tm
    return pl.pallas_call(
        functools.partial(_pool_kernel, tiles_per_seq=tiles_per_seq),
        grid=(t // tm,),
        in_specs=[pl.BlockSpec((tm, width), lambda i: (i, cb)),
                  pl.BlockSpec((tm, width), lambda i: (jnp.maximum(i - 1, 0), cb)),
                  pl.BlockSpec((len(POOL_WINDOWS), POOL_GC, POOL_GC), lambda i: (0, 0, 0)),
                  pl.BlockSpec((1, width), lambda i: (0, 0))],
        out_specs=pl.BlockSpec((tm, width), lambda i: (i, 0)),
        out_shape=jax.ShapeDtypeStruct((t, width), BF16),
        compiler_params=_params(1),
        name="pool_mixer",
    )(proj, proj, pool_w.astype(BF16), pool_scale.reshape(1, width))


def _swa_kernel(q_ref, kc_ref, kp_ref, vc_ref, vp_ref, sink_ref, o_ref, *, blocks_per_seq):
    blk = SWA_BLOCK
    nk = 2 * blk
    rep = SWA_HQ // SWA_HKV
    ppg = rep // 2
    i = pl.program_id(0)
    not_first = (i % blocks_per_seq) > 0
    lane = lax.broadcasted_iota(jnp.int32, (nk, LANES), 1)

    kk = jnp.concatenate([kp_ref[...], kc_ref[...]], axis=0).astype(F32)
    k0_lo = jnp.where(lane < HEAD_DIM, kk, 0.0)
    k1_hi = jnp.where(lane >= HEAD_DIM, kk, 0.0)
    kmats = ((k0_lo.astype(BF16), pltpu.roll(k0_lo, HEAD_DIM, 1).astype(BF16)),
             (pltpu.roll(k1_hi, HEAD_DIM, 1).astype(BF16), k1_hi.astype(BF16)))
    vt = jnp.concatenate([vp_ref[...], vc_ref[...]], axis=0).astype(F32).T
    ones_rows = (lax.broadcasted_iota(jnp.int32, (VT_ROWS - HEAD_DIM, nk), 0) == 0).astype(F32)
    vts = [jnp.concatenate([vt[g * HEAD_DIM:(g + 1) * HEAD_DIM], ones_rows], axis=0).astype(BF16)
           for g in range(SWA_HKV)]
    qg = [jnp.concatenate([q_ref[:, (g * ppg + pp) * LANES:(g * ppg + pp + 1) * LANES] for pp in range(ppg)], axis=0)
          for g in range(SWA_HKV)]

    ncol = ppg * blk
    key = lax.broadcasted_iota(jnp.int32, (nk, ncol), 0)
    col = lax.broadcasted_iota(jnp.int32, (nk, ncol), 1)
    dist = col % blk + blk - key
    band = (dist >= 0) & (dist < blk) & (not_first | (key >= blk))
    seg = lax.broadcasted_iota(jnp.int32, (1, ncol), 1) // blk

    tasks = [(g, par) for g in range(SWA_HKV) for par in range(2)]

    def scores(g, par):
        return lax.dot_general(kmats[g][par], qg[g], _NT, preferred_element_type=F32)

    outs = {}
    st_next = scores(*tasks[0])
    for n, (g, par) in enumerate(tasks):
        st = st_next
        if n + 1 < len(tasks):
            st_next = scores(*tasks[n + 1])
        sink = jnp.zeros((1, ncol), F32)
        for pp in range(ppg):
            sink = jnp.where(seg == pp, sink_ref[0, g * rep + 2 * pp + par], sink)
        sb = jnp.where(band, st, NEG_INF).astype(BF16)
        mb = jnp.maximum(jnp.max(sb, axis=0, keepdims=True), sink.astype(BF16))
        pm = jnp.exp(sb - mb)
        pv = jnp.dot(vts[g], pm, preferred_element_type=F32)
        den = pv[HEAD_DIM:HEAD_DIM + 1] + jnp.exp(sink - mb.astype(F32))
        outs[(g, par)] = pv[0:HEAD_DIM] / den
    for p in range(SWA_HQ // 2):
        g, pp = p // ppg, p % ppg
        pair_t = jnp.concatenate([outs[(g, 0)][:, pp * blk:(pp + 1) * blk], outs[(g, 1)][:, pp * blk:(pp + 1) * blk]],
                                 axis=0)
        o_ref[:, p * LANES:(p + 1) * LANES] = pair_t.T.astype(o_ref.dtype)


def swa_attention(q, k, proj, v_col, sinks, seq):
    t = q.shape[0]
    blk = SWA_BLOCK
    bps = seq // blk
    kvw = SWA_HKV * HEAD_DIM
    vb = v_col // kvw

    def prev(i):
        return jnp.maximum(i - 1, 0)

    return pl.pallas_call(
        functools.partial(_swa_kernel, blocks_per_seq=bps),
        grid=(t // blk,),
        in_specs=[pl.BlockSpec((blk, q.shape[1]), lambda i: (i, 0)),
                  pl.BlockSpec((blk, kvw), lambda i: (i, 0)),
                  pl.BlockSpec((blk, kvw), lambda i: (prev(i), 0)),
                  pl.BlockSpec((blk, kvw), lambda i: (i, vb)),
                  pl.BlockSpec((blk, kvw), lambda i: (prev(i), vb)),
                  pl.BlockSpec((1, SWA_HQ), lambda i: (0, 0), memory_space=pltpu.SMEM)],
        out_specs=pl.BlockSpec((blk, q.shape[1]), lambda i: (i, 0)),
        out_shape=jax.ShapeDtypeStruct(q.shape, BF16),
        compiler_params=_params(1),
        name="swa_attention",
    )(q, k, k, proj, proj, sinks.reshape(1, SWA_HQ).astype(F32))


VT_ROWS = HEAD_DIM + 16


def _moba_kernel(q_ref, k_ref, v_ref, km_ref, o_ref, vt_ref, ke_ref, ko_ref, m_ref, acc_ref,
                 *, nblk, qt):
    blk = MOBA_BLOCK
    sub = qt // blk
    npair = q_ref.shape[1] // LANES
    qi = pl.program_id(2)
    lane = lax.broadcasted_iota(jnp.int32, (blk, LANES), 1)
    low = lane < HEAD_DIM

    @pl.when(qi == 0)
    def _():
        ones_row = (lax.broadcasted_iota(jnp.int32, (VT_ROWS - HEAD_DIM, vt_ref.shape[2]), 0) == 0).astype(BF16)
        for h in range(2 * npair):
            vt_ref[h, HEAD_DIM:VT_ROWS, :] = ones_row

        def fill(j, c):
            off = pl.multiple_of(j * blk, blk)
            vt = v_ref[pl.ds(off, blk), :].astype(F32).T.astype(BF16)
            for h in range(2 * npair):
                vt_ref[h, 0:HEAD_DIM, pl.ds(off, blk)] = vt[h * HEAD_DIM:(h + 1) * HEAD_DIM, :]
            for p in range(npair):
                kf = k_ref[pl.ds(off, blk), p * LANES:(p + 1) * LANES].astype(F32)
                ke_ref[p, pl.ds(off, blk), :] = jnp.where(low, kf, (lane - HEAD_DIM == j).astype(F32)).astype(BF16)
                ko_ref[p, pl.ds(off, blk), :] = jnp.where(low, (lane == j).astype(F32), kf).astype(BF16)
            return c
        lax.fori_loop(0, nblk, fill, 0)

    block_of_row = lax.broadcasted_iota(jnp.int32, (HEAD_DIM, blk), 0)

    def block_bias(gates_t, own):
        g = jnp.where(block_of_row < own, gates_t, -jnp.inf)
        sel = block_of_row == own
        for _ in range(MOBA_TOPK):
            mx = jnp.max(g, axis=0, keepdims=True)
            idx = jnp.min(jnp.where(g == mx, block_of_row, 1 << 20), axis=0, keepdims=True)
            hit = (block_of_row == idx) & (mx > -jnp.inf)
            sel = sel | hit
            g = jnp.where(hit, -jnp.inf, g)
        return jnp.where(sel, 0.0, NEG_INF)

    q_ext = []
    for p in range(npair):
        ext_e, ext_o = [], []
        for c in range(sub):
            own = qi * sub + c
            qp = q_ref[c * blk:(c + 1) * blk, p * LANES:(p + 1) * LANES]
            gates_t = lax.dot_general(km_ref[p], qp, _NT, preferred_element_type=F32)
            bias_t = jnp.concatenate([block_bias(gates_t[0:HEAD_DIM], own), block_bias(gates_t[HEAD_DIM:], own)],
                                     axis=0)
            bias = bias_t.T
            qf = qp.astype(F32)
            ext_e.append(jnp.where(low, qf, bias).astype(BF16))
            ext_o.append(jnp.where(low, bias, qf).astype(BF16))
        q_ext.append((jnp.concatenate(ext_e, axis=0), jnp.concatenate(ext_o, axis=0)))

    m_ref[...] = jnp.full(m_ref.shape, -jnp.inf, F32)
    acc_ref[...] = jnp.zeros(acc_ref.shape, F32)
    rows = sub * blk
    key_idx = lax.broadcasted_iota(jnp.int32, (rows, qt), 0)
    qry_idx = lax.broadcasted_iota(jnp.int32, (rows, qt), 1)

    def steps(jts, causal):
        tasks = [(pl.multiple_of(jt * rows, rows), h) for jt in jts for h in range(2 * npair)]

        def scores(off, h):
            k_ext = (ke_ref if h % 2 == 0 else ko_ref)[h // 2, pl.ds(off, rows), :]
            return lax.dot_general(k_ext, q_ext[h // 2][h % 2], _NT, preferred_element_type=F32)

        st_next = scores(*tasks[0])
        for n, (off, h) in enumerate(tasks):
            st = st_next
            if n + 1 < len(tasks):
                st_next = scores(*tasks[n + 1])
            if causal:
                st = jnp.where(key_idx <= qry_idx, st, NEG_INF)
            sb = st.astype(BF16)
            m = m_ref[h]
            m_new = jnp.maximum(m, jnp.max(sb, axis=0, keepdims=True).astype(F32))
            alpha = jnp.exp(m - m_new)
            pm = jnp.exp(sb - m_new.astype(BF16))
            pv = jnp.dot(vt_ref[h, :, pl.ds(off, rows)], pm, preferred_element_type=F32)
            acc_ref[h] = acc_ref[h] * alpha + pv
            m_ref[h] = m_new

    def body(u, c):
        steps([2 * u, 2 * u + 1], False)
        return c

    lax.fori_loop(0, qi // 2, body, 0)

    @pl.when(qi % 2 == 1)
    def _():
        steps([qi - 1], False)

    steps([qi], True)
    outs = [acc_ref[h, 0:HEAD_DIM, :] / acc_ref[h, HEAD_DIM:HEAD_DIM + 1, :] for h in range(2 * npair)]
    o_ref[...] = jnp.concatenate(outs, axis=0).T.astype(o_ref.dtype)


def moba_attention(q, k, proj, v_col, gate_mats, batch, seq, qt=512, npair=2):
    t, width = q.shape
    nblk = seq // MOBA_BLOCK
    qt = min(qt, seq)
    gw = npair * LANES
    q3 = q.reshape(batch, seq, width)
    k3 = k.reshape(batch, seq, width)
    p3 = proj.reshape(batch, seq, proj.shape[1])
    vb = v_col // gw

    def resident(shape, index_map):
        return pl.BlockSpec(shape, index_map, pipeline_mode=pl.Buffered(1))

    out = pl.pallas_call(
        functools.partial(_moba_kernel, nblk=nblk, qt=qt),
        grid=(batch, width // gw, seq // qt),
        in_specs=[pl.BlockSpec((None, qt, gw), lambda b, g, i: (b, i, g)),
                  resident((None, seq, gw), lambda b, g, i: (b, 0, g)),
                  resident((None, seq, gw), lambda b, g, i: (b, 0, vb + g)),
                  pl.BlockSpec((None, npair, LANES, LANES), lambda b, g, i: (b, g, 0, 0))],
        out_specs=pl.BlockSpec((None, qt, gw), lambda b, g, i: (b, i, g)),
        out_shape=jax.ShapeDtypeStruct((batch, seq, width), BF16),
        scratch_shapes=[pltpu.VMEM((2 * npair, VT_ROWS, seq), BF16),
                        pltpu.VMEM((npair, seq, LANES), BF16), pltpu.VMEM((npair, seq, LANES), BF16),
                        pltpu.VMEM((2 * npair, 1, qt), F32),
                        pltpu.VMEM((2 * npair, VT_ROWS, qt), F32)],
        compiler_params=_params(3),
        name="moba_attention",
    )(q3, k3, p3, gate_mats)
    return out.reshape(t, width)


def moba_gate_mats(kmean, batch, nblk):
    km = kmean.reshape(batch, nblk, MOBA_HEADS // 2, 2, HEAD_DIM).astype(BF16)
    km = jnp.transpose(km, (0, 2, 3, 1, 4))
    pad_rows = LANES // 2 - nblk
    even = jnp.pad(km[:, :, 0], ((0, 0), (0, 0), (LANES // 2, pad_rows), (0, HEAD_DIM)))
    odd = jnp.pad(km[:, :, 1], ((0, 0), (0, 0), (0, LANES // 2 + pad_rows), (HEAD_DIM, 0)))
    return even + odd


def _outproj_kernel(x_ref, ya_ref, yb_ref, yc_ref, ga_ref, gb_ref, gc_ref,
                    wa_ref, wb_ref, wc_ref, wo_ref, gain_ref, xo_ref, h_ref):
    def gate(g_ref):
        return jax.nn.sigmoid(g_ref[...].astype(F32))

    merged = gate(ga_ref) * jnp.dot(ya_ref[...], wa_ref[...], preferred_element_type=F32)
    merged = merged + gate(gb_ref) * jnp.dot(yb_ref[...], wb_ref[...], preferred_element_type=F32)
    merged = merged + gate(gc_ref) * jnp.dot(yc_ref[...], wc_ref[...], preferred_element_type=F32)
    xn = x_ref[...] + jnp.dot(merged.astype(BF16), wo_ref[...], preferred_element_type=F32)
    xo_ref[...] = xn
    ms = jnp.mean(xn * xn, axis=-1, keepdims=True)
    h_ref[...] = xn * lax.rsqrt(ms + RMS_EPS) * gain_ref[...]


def out_proj(x2d, ya, yb, yc, proj, gate_col, wa, wb, wc, wo, gain, tm=256):
    t, d = x2d.shape
    gb0 = gate_col // d

    def rows(width):
        return pl.BlockSpec((tm, width), lambda i: (i, 0))

    def whole(w):
        return pl.BlockSpec(w.shape, lambda i: (0, 0), pipeline_mode=pl.Buffered(1))

    return pl.pallas_call(
        _outproj_kernel,
        grid=(t // tm,),
        in_specs=[rows(d), rows(ya.shape[1]), rows(yb.shape[1]), rows(yc.shape[1]),
                  pl.BlockSpec((tm, d), lambda i: (i, gb0)),
                  pl.BlockSpec((tm, d), lambda i: (i, gb0 + 1)),
                  pl.BlockSpec((tm, d), lambda i: (i, gb0 + 2)),
                  whole(wa), whole(wb), whole(wc), whole(wo),
                  pl.BlockSpec((1, d), lambda i: (0, 0))],
        out_specs=[rows(d), rows(d)],
        out_shape=[jax.ShapeDtypeStruct((t, d), F32), jax.ShapeDtypeStruct((t, d), F32)],
        compiler_params=_params(1),
        name="out_proj",
    )(x2d, ya, yb, yc, proj, proj, proj, wa, wb, wc, wo, gain.reshape(1, d))


def _router_kernel(h_ref, wr_ref, route_ref, count_ref, run_ref):
    tm = h_ref.shape[0]
    epg = EXPERTS_PER_GROUP

    @pl.when(pl.program_id(0) == 0)
    def _():
        run_ref[...] = jnp.zeros(run_ref.shape, F32)

    logits = lax.dot_general(wr_ref[...], h_ref[...].astype(BF16), _NT, preferred_element_type=F32)
    row8 = lax.broadcasted_iota(jnp.int32, (epg, tm), 0)
    glog = jnp.where(row8 < N_GROUPS, logits[N_EXPERTS:N_EXPERTS + epg, :], -jnp.inf)
    gmax = jnp.max(glog, axis=0, keepdims=True)
    gsel = jnp.min(jnp.where(glog == gmax, row8, epg), axis=0, keepdims=True)
    p_grp = 1.0 / jnp.sum(jnp.exp(glog - gmax), axis=0, keepdims=True)
    within = jnp.zeros((epg, tm), F32)
    for g in range(N_GROUPS):
        within = jnp.where(gsel == g, logits[g * epg:(g + 1) * epg, :], within)
    wmax = jnp.max(within, axis=0, keepdims=True)
    e = jnp.exp(within - wmax)
    prob = e / jnp.sum(e, axis=0, keepdims=True)
    p1 = jnp.max(prob, axis=0, keepdims=True)
    i1 = jnp.min(jnp.where(prob == p1, row8, epg), axis=0, keepdims=True)
    rest = jnp.where(row8 == i1, -jnp.inf, prob)
    p2 = jnp.max(rest, axis=0, keepdims=True)
    i2 = jnp.min(jnp.where(rest == p2, row8, epg), axis=0, keepdims=True)
    norm = p1 + p2
    w1 = p_grp * p1 / norm
    w2 = p_grp * p2 / norm
    e1 = gsel * epg + i1
    e2 = gsel * epg + i2

    row32 = lax.broadcasted_iota(jnp.int32, (N_EXPERTS, tm), 0)
    oh1 = (row32 == e1).astype(F32)
    oh2 = (row32 == e2).astype(F32)
    oh = oh1 + oh2
    before = (lax.broadcasted_iota(jnp.int32, (tm, tm), 0) < lax.broadcasted_iota(jnp.int32, (tm, tm), 1)).astype(BF16)
    prior = jnp.dot(oh.astype(BF16), before, preferred_element_type=F32) + run_ref[:, 0:1]
    r1 = jnp.sum(oh1 * prior, axis=0, keepdims=True)
    r2 = jnp.sum(oh2 * prior, axis=0, keepdims=True)
    run_new = run_ref[...] + jnp.sum(oh, axis=1, keepdims=True)
    run_ref[...] = run_new
    count_ref[...] = run_new
    zero = jnp.zeros((1, tm), F32)
    route_ref[...] = jnp.concatenate([e1.astype(F32), e2.astype(F32), r1, r2, w1, w2, zero, zero], axis=0)


def router(h, wr_t, tm=512):
    t, d = h.shape
    return pl.pallas_call(
        _router_kernel,
        grid=(t // tm,),
        in_specs=[pl.BlockSpec((tm, d), lambda i: (i, 0)), pl.BlockSpec(wr_t.shape, lambda i: (0, 0))],
        out_specs=[pl.BlockSpec((8, tm), lambda i: (0, i)), pl.BlockSpec((N_EXPERTS, LANES), lambda i: (0, 0))],
        out_shape=[jax.ShapeDtypeStruct((8, t), F32), jax.ShapeDtypeStruct((N_EXPERTS, LANES), F32)],
        scratch_shapes=[pltpu.VMEM((N_EXPERTS, LANES), F32)],
        compiler_params=_params(1),
        name="router",
    )(h, wr_t)


def _row_copy(src_ref, src_row, dst_ref, dst_row, sem):
    return pltpu.make_async_copy(src_ref.at[pl.ds(src_row, 1)], dst_ref.at[pl.ds(dst_row, 1)], sem)


def _dispatch_kernel(dest_ref, h_ref, init_ref, xs_ref, sem):
    del init_ref
    tm = h_ref.shape[0]

    def start(r, c):
        for k in range(2):
            _row_copy(h_ref, r, xs_ref, dest_ref[k, r], sem).start()
        return c

    def wait(r, c):
        for k in range(2):
            _row_copy(h_ref, r, xs_ref, dest_ref[k, r], sem).wait()
        return c

    lax.fori_loop(0, tm, start, 0, unroll=8)
    lax.fori_loop(0, tm, wait, 0, unroll=8)


def dispatch(h, dest, n_slots, tm=256):
    t, d = h.shape
    dest3 = dest.reshape(2, t // tm, tm).transpose(1, 0, 2)
    return pl.pallas_call(
        _dispatch_kernel,
        grid=(t // tm,),
        in_specs=[pl.BlockSpec((None, 2, tm), lambda i: (i, 0, 0), memory_space=pltpu.SMEM),
                  pl.BlockSpec((tm, d), lambda i: (i, 0)),
                  pl.BlockSpec(memory_space=pl.ANY)],
        out_specs=pl.BlockSpec(memory_space=pl.ANY),
        out_shape=jax.ShapeDtypeStruct((n_slots, d), h.dtype),
        input_output_aliases={2: 0},
        scratch_shapes=[pltpu.SemaphoreType.DMA],
        compiler_params=_params(1),
        name="moe_dispatch",
    )(dest3, h, jnp.zeros((n_slots, d), h.dtype))


CAST_ROWS = 256


def _cast_rows(src_ref, dst_ref):
    def body(c, carry):
        sl = pl.ds(pl.multiple_of(c * CAST_ROWS, CAST_ROWS), CAST_ROWS)
        dst_ref[sl, :] = src_ref[sl, :].astype(dst_ref.dtype)
        return carry
    lax.fori_loop(0, src_ref.shape[0] // CAST_ROWS, body, 0)


def _expert_kernel(bexp_ref, nvalid_ref, xs_ref, wg_ref, wu_ref, wd_ref, ys_ref, wg_bf, wu_bf, wd_bf, loaded_ref):
    n = pl.program_id(0)
    nv = nvalid_ref[n]

    @pl.when(n == 0)
    def _():
        loaded_ref[0] = -1

    @pl.when((nv > 0) & (loaded_ref[0] != bexp_ref[n]))
    def _():
        _cast_rows(wg_ref, wg_bf)
        _cast_rows(wu_ref, wu_bf)
        _cast_rows(wd_ref, wd_bf)
        loaded_ref[0] = bexp_ref[n]

    @pl.when(nv > 0)
    def _():
        row = lax.broadcasted_iota(jnp.int32, xs_ref.shape, 0)
        x = jnp.where(row < nv, xs_ref[...], 0.0).astype(BF16)
        g = jnp.dot(x, wg_bf[...], preferred_element_type=F32)
        u = jnp.dot(x, wu_bf[...], preferred_element_type=F32)
        act = (jax.nn.silu(g) * u).astype(BF16)
        ys_ref[...] = jnp.dot(act, wd_bf[...], preferred_element_type=F32)

    @pl.when(nv <= 0)
    def _():
        ys_ref[...] = jnp.zeros(ys_ref.shape, ys_ref.dtype)


def experts(xs, blk_exp, nvalid, wg, wu, wd, layer):
    n_slots, d = xs.shape
    ff = wg.shape[3]
    nb = n_slots // MOE_BLOCK

    def weight(shape):
        return pl.BlockSpec((None, None) + shape, lambda n, be, nv: (layer, be[n], 0, 0),
                            pipeline_mode=pl.Buffered(1))

    grid_spec = pltpu.PrefetchScalarGridSpec(
        num_scalar_prefetch=2,
        grid=(nb,),
        in_specs=[pl.BlockSpec((MOE_BLOCK, d), lambda n, be, nv: (n, 0)),
                  weight((d, ff)), weight((d, ff)), weight((ff, d))],
        out_specs=pl.BlockSpec((MOE_BLOCK, d), lambda n, be, nv: (n, 0)),
        scratch_shapes=[pltpu.VMEM((d, ff), BF16), pltpu.VMEM((d, ff), BF16), pltpu.VMEM((ff, d), BF16),
                        pltpu.SMEM((1,), jnp.int32)],
    )
    return pl.pallas_call(
        _expert_kernel,
        grid_spec=grid_spec,
        out_shape=jax.ShapeDtypeStruct((n_slots, d), F32),
        compiler_params=_params(1),
        name="moe_experts",
    )(blk_exp, nvalid, xs, wg, wu, wd)


def _combine_kernel(dest_ref, x_ref, w_ref, gain_ref, ys_ref, xo_ref, *rest):
    h_ref, buf_ref, sem = rest if len(rest) == 3 else (None,) + rest
    tm = x_ref.shape[0]

    def start(r, c):
        for k in range(2):
            _row_copy(ys_ref, dest_ref[k, r], buf_ref.at[k], r, sem).start()
        return c

    def wait(r, c):
        for k in range(2):
            _row_copy(ys_ref, dest_ref[k, r], buf_ref.at[k], r, sem).wait()
        return c

    lax.fori_loop(0, tm, start, 0, unroll=8)
    lax.fori_loop(0, tm, wait, 0, unroll=8)
    w = w_ref[...]
    xn = x_ref[...] + buf_ref[0] * w[:, 0:1] + buf_ref[1] * w[:, 1:2]
    xo_ref[...] = xn
    if h_ref is not None:
        ms = jnp.mean(xn * xn, axis=-1, keepdims=True)
        h_ref[...] = (xn * lax.rsqrt(ms + RMS_EPS) * gain_ref[...]).astype(h_ref.dtype)


def combine(x2d, ys, dest, wts, gain, emit_norm, tm=256):
    t, d = x2d.shape
    dest3 = dest.reshape(2, t // tm, tm).transpose(1, 0, 2)
    return pl.pallas_call(
        _combine_kernel,
        grid=(t // tm,),
        in_specs=[pl.BlockSpec((None, 2, tm), lambda i: (i, 0, 0), memory_space=pltpu.SMEM),
                  pl.BlockSpec((tm, d), lambda i: (i, 0)),
                  pl.BlockSpec((tm, 2), lambda i: (i, 0)),
                  pl.BlockSpec((1, d), lambda i: (0, 0)),
                  pl.BlockSpec(memory_space=pl.ANY)],
        out_specs=[pl.BlockSpec((tm, d), lambda i: (i, 0))] * (2 if emit_norm else 1),
        out_shape=[jax.ShapeDtypeStruct((t, d), F32)] + ([jax.ShapeDtypeStruct((t, d), BF16)] if emit_norm else []),
        scratch_shapes=[pltpu.VMEM((2, tm, d), F32), pltpu.SemaphoreType.DMA],
        compiler_params=_params(1),
        name="moe_combine",
    )(dest3, x2d, wts, gain.reshape(1, d), ys)


def _rope_tables(seq):
    inv_freq = 1.0 / (ROPE_THETA ** (jnp.arange(0, HEAD_DIM, 2, dtype=F32) / HEAD_DIM))
    ang = jnp.arange(seq, dtype=F32)[:, None] * inv_freq[None, :]
    cos, sin = jnp.cos(ang), jnp.sin(ang)
    cos_t = jnp.concatenate([cos, cos, cos, cos], axis=1)
    sin_t = jnp.concatenate([-sin, sin, -sin, sin], axis=1)
    return cos_t, sin_t


def _moe_plan(route, counts, n_tokens):
    eid = route[0:2].astype(jnp.int32)
    rank = route[2:4].astype(jnp.int32)
    wts = route[4:6].T
    cnt = counts[:, 0].astype(jnp.int32)
    padded = ((cnt + MOE_BLOCK - 1) // MOE_BLOCK) * MOE_BLOCK
    p_ends = jnp.cumsum(padded)
    p_starts = p_ends - padded
    hit = eid[..., None] == jnp.arange(N_EXPERTS, dtype=jnp.int32)
    dest = rank + jnp.sum(jnp.where(hit, p_starts, 0), axis=-1)
    n_blocks = -(-(n_tokens * 2) // MOE_BLOCK) + N_EXPERTS
    blk_start = jnp.arange(n_blocks, dtype=jnp.int32) * MOE_BLOCK
    blk_exp = jnp.minimum(jnp.sum(p_ends[None, :] <= blk_start[:, None], axis=1), N_EXPERTS - 1).astype(jnp.int32)
    nvalid = jnp.clip(cnt[blk_exp] - (blk_start - p_starts[blk_exp]), 0, MOE_BLOCK).astype(jnp.int32)
    return dest, wts, blk_exp, nvalid, n_blocks * MOE_BLOCK


def kernel(x, norm_mix, w_in, pool_w, pool_scale, swa_q_norm, swa_k_norm, swa_sinks, moba_q_norm, moba_k_norm,
           w_up_a, w_up_b, w_up_c, w_o, norm_ffn, w_router_group, w_router_expert, w_exp_gate, w_exp_up, w_exp_down):
    batch, seq, d = x.shape
    t = batch * seq
    depth = w_in.shape[0]
    pool_width = len(POOL_WINDOWS) * POOL_GC
    wqb, wkv, wm = SWA_HQ * HEAD_DIM, SWA_HKV * HEAD_DIM, MOBA_HEADS * HEAD_DIM

    names = ["xa", "qb", "kb", "vb", "qm", "km", "vm", "ga", "gb", "gc"]
    widths = [pool_width, wqb, wkv, wkv, wm, wm, wm, d, d, d]
    src, o = {}, 0
    for nme, w in zip(names, widths):
        src[nme] = (o, w)
        o += w
    order = ["ga", "gb", "gc", "qb", "qm", "km", "vm", "xa", "kb", "vb"]
    cols, o = {}, 0
    for nme in order:
        cols[nme] = o
        o += src[nme][1]
    tile_map = [0] * (o // IN_PROJ_TN)
    for nme in order:
        if nme == "vb":
            continue
        width = src[nme][1] + (src["vb"][1] if nme == "kb" else 0)
        for k in range(width // IN_PROJ_TN):
            tile_map[src[nme][0] // IN_PROJ_TN + k] = cols[nme] // IN_PROJ_TN + k
    tile_map = jnp.asarray(tile_map, jnp.int32)

    cos_t, sin_t = _rope_tables(seq)
    x2d = x.reshape(t, d)
    h = rmsnorm_cast(x2d, norm_mix[0])

    for l in range(depth):
        proj = in_proj(h, w_in, l, tile_map)

        def tile2(g):
            return jnp.concatenate([g, g], axis=0)

        gains = jnp.stack([tile2(swa_q_norm[l]), tile2(swa_k_norm[l]), tile2(moba_q_norm[l]), tile2(moba_k_norm[l])]
                          + [jnp.zeros((LANES,), F32)] * 4)
        qb, kb, qm, km, kmean = qk_prep(proj, cols, cos_t, sin_t, gains, seq)
        y_a = pool_mixer(proj, cols["xa"], pool_w[l], pool_scale[l], seq)
        y_b = swa_attention(qb, kb, proj, cols["vb"], swa_sinks[l], seq)
        gate_mats = moba_gate_mats(kmean, batch, seq // MOBA_BLOCK)
        y_c = moba_attention(qm, km, proj, cols["vm"], gate_mats, batch, seq)
        x2d, h_ffn = out_proj(x2d, y_a, y_b, y_c, proj, cols["ga"],
                              w_up_a[l].astype(BF16), w_up_b[l].astype(BF16), w_up_c[l].astype(BF16),
                              w_o[l].astype(BF16), norm_ffn[l])

        wr_t = jnp.concatenate([w_router_expert[l].T, w_router_group[l].T,
                                jnp.zeros((LANES - N_EXPERTS - N_GROUPS, d), F32)], axis=0).astype(BF16)
        route, counts = router(h_ffn, wr_t)
        dest, wts, blk_exp, nvalid, n_slots = _moe_plan(route, counts, t)
        xs = dispatch(h_ffn, dest, n_slots)
        ys = experts(xs, blk_exp, nvalid, w_exp_gate, w_exp_up, w_exp_down, l)
        last = l + 1 == depth
        outs = combine(x2d, ys, dest, wts, norm_mix[0 if last else l + 1], emit_norm=not last)
        x2d = outs[0]
        h = None if last else outs[1]
    return x2d.reshape(batch, seq, d)
```

```python
import functools

import jax
import jax.numpy as jnp
from jax import lax
from jax.experimental import pallas as pl
from jax.experimental.pallas import tpu as pltpu

F32 = jnp.float32
BF16 = jnp.bfloat16

HEAD_DIM = 64
ROPE_THETA = 10000.0
RMS_EPS = 1e-6
NEG_INF = -1e30

POOL_WINDOWS = (2, 4, 8, 16)
POOL_GC = 128

SWA_HQ = 16
SWA_HKV = 2
SWA_BLOCK = 128

MOBA_HEADS = 8
MOBA_BLOCK = 256
MOBA_TOPK = 3

N_GROUPS = 4
EXPERTS_PER_GROUP = 8
N_EXPERTS = 32
MOE_BLOCK = 256

LANES = 128
VMEM_LIMIT = 56 * 1024 * 1024

_NT = (((1,), (1,)), ((), ()))


def _params(n_axes, vmem=VMEM_LIMIT):
    return pltpu.CompilerParams(dimension_semantics=("arbitrary",) * n_axes, vmem_limit_bytes=vmem)


def _rmsnorm_kernel(x_ref, g_ref, o_ref):
    x = x_ref[...]
    ms = jnp.mean(x * x, axis=-1, keepdims=True)
    o_ref[...] = (x * lax.rsqrt(ms + RMS_EPS) * g_ref[...]).astype(o_ref.dtype)


def rmsnorm_cast(x2d, gain, tm=512):
    t, d = x2d.shape
    return pl.pallas_call(
        _rmsnorm_kernel,
        grid=(t // tm,),
        in_specs=[pl.BlockSpec((tm, d), lambda i: (i, 0)), pl.BlockSpec((1, d), lambda i: (0, 0))],
        out_specs=pl.BlockSpec((tm, d), lambda i: (i, 0)),
        out_shape=jax.ShapeDtypeStruct((t, d), BF16),
        compiler_params=_params(1),
        name="rmsnorm",
    )(x2d, gain.reshape(1, d))


IN_PROJ_TN = 256


def _in_proj_kernel(tile_map_ref, a_ref, w_ref, o_ref):
    del tile_map_ref
    o_ref[...] = jnp.dot(a_ref[...], w_ref[...].astype(BF16), preferred_element_type=F32).astype(o_ref.dtype)


def in_proj(h, w_in, layer, tile_map, tm=2048):
    t, k = h.shape
    n = w_in.shape[2]
    tn = IN_PROJ_TN
    tm = min(tm, t)
    grid_spec = pltpu.PrefetchScalarGridSpec(
        num_scalar_prefetch=1,
        grid=(t // tm, n // tn),
        in_specs=[pl.BlockSpec((tm, k), lambda i, j, tmap: (i, 0)),
                  pl.BlockSpec((None, k, tn), lambda i, j, tmap: (layer, 0, j))],
        out_specs=pl.BlockSpec((tm, tn), lambda i, j, tmap: (i, tmap[j])),
    )
    return pl.pallas_call(
        _in_proj_kernel,
        grid_spec=grid_spec,
        out_shape=jax.ShapeDtypeStruct((t, n), BF16),
        compiler_params=_params(2),
        name="in_proj",
    )(tile_map, h, w_in)


def _norm_rope(x, gain, cos, sin_signed, blockdiag, first_half):
    x2 = x * x
    hi = x2.astype(BF16)
    lo = (x2 - hi.astype(F32)).astype(BF16)
    ss = jnp.dot(hi, blockdiag, preferred_element_type=F32) + jnp.dot(lo, blockdiag, preferred_element_type=F32)
    y = x * lax.rsqrt(ss * (1.0 / HEAD_DIM) + RMS_EPS) * gain
    rot = jnp.where(first_half, pltpu.roll(y, LANES - HEAD_DIM // 2, 1), pltpu.roll(y, HEAD_DIM // 2, 1))
    return y * cos + rot * sin_signed


def _prep_kernel(qb_ref, kb_ref, qm_ref, km_ref, cos_ref, sin_ref, gains_ref,
                 qb_o, kb_o, qm_o, km_o, kmean_o, *, scale):
    tm = qb_ref.shape[0]
    cos = cos_ref[...]
    sin = sin_ref[...]
    lane = lax.broadcasted_iota(jnp.int32, (tm, LANES), 1)
    first_half = (lane % HEAD_DIM) < (HEAD_DIM // 2)
    r = lax.broadcasted_iota(jnp.int32, (LANES, LANES), 0) // HEAD_DIM
    c = lax.broadcasted_iota(jnp.int32, (LANES, LANES), 1) // HEAD_DIM
    blockdiag = (r == c).astype(BF16)

    def run(src, dst, gain_row, mult, n_groups):
        outs = []
        gain = gains_ref[gain_row:gain_row + 1, :]
        for g in range(n_groups):
            sl = slice(g * LANES, (g + 1) * LANES)
            y = _norm_rope(src[:, sl].astype(F32), gain, cos, sin, blockdiag, first_half)
            if mult != 1.0:
                y = y * mult
            dst[:, sl] = y.astype(dst.dtype)
            outs.append(y)
        return outs

    run(qb_ref, qb_o, 0, scale, qb_ref.shape[1] // LANES)
    run(kb_ref, kb_o, 1, 1.0, kb_ref.shape[1] // LANES)
    run(qm_ref, qm_o, 2, scale, qm_ref.shape[1] // LANES)
    kms = run(km_ref, km_o, 3, 1.0, km_ref.shape[1] // LANES)
    for g, y in enumerate(kms):
        for blk in range(tm // MOBA_BLOCK):
            s = jnp.sum(y[blk * MOBA_BLOCK:(blk + 1) * MOBA_BLOCK, :], axis=0, keepdims=True)
            kmean_o[blk, :, g * LANES:(g + 1) * LANES] = s * (1.0 / MOBA_BLOCK)


def qk_prep(proj, cols, cos_t, sin_t, gains, seq, tm=512):
    t = proj.shape[0]
    tiles_per_seq = seq // tm
    wqb, wkb, wqm, wkm = SWA_HQ * HEAD_DIM, SWA_HKV * HEAD_DIM, MOBA_HEADS * HEAD_DIM, MOBA_HEADS * HEAD_DIM

    def col_spec(width, off):
        return pl.BlockSpec((tm, width), lambda i, o=off // width: (i, o))

    tab_spec = pl.BlockSpec((tm, LANES), lambda i: (i % tiles_per_seq, 0))
    return pl.pallas_call(
        functools.partial(_prep_kernel, scale=HEAD_DIM ** -0.5),
        grid=(t // tm,),
        in_specs=[col_spec(wqb, cols["qb"]), col_spec(wkb, cols["kb"]), col_spec(wqm, cols["qm"]),
                  col_spec(wkm, cols["km"]), tab_spec, tab_spec, pl.BlockSpec((8, LANES), lambda i: (0, 0))],
        out_specs=[pl.BlockSpec((tm, wqb), lambda i: (i, 0)), pl.BlockSpec((tm, wkb), lambda i: (i, 0)),
                   pl.BlockSpec((tm, wqm), lambda i: (i, 0)), pl.BlockSpec((tm, wkm), lambda i: (i, 0)),
                   pl.BlockSpec((tm // MOBA_BLOCK, 1, wkm), lambda i: (i, 0, 0))],
        out_shape=[jax.ShapeDtypeStruct((t, wqb), BF16), jax.ShapeDtypeStruct((t, wkb), BF16),
                   jax.ShapeDtypeStruct((t, wqm), BF16), jax.ShapeDtypeStruct((t, wkm), BF16),
                   jax.ShapeDtypeStruct((t // MOBA_BLOCK, 1, wkm), F32)],
        compiler_params=_params(1),
        name="qk_prep",
    )(proj, proj, proj, proj, cos_t, sin_t, gains)


def _pool_kernel(cur_ref, prev_ref, pw_ref, ps_ref, o_ref, *, tiles_per_seq):
    tm = cur_ref.shape[0]
    i = pl.program_id(0)
    tile_in_seq = i % tiles_per_seq
    row = lax.broadcasted_iota(jnp.int32, (tm, tm), 0)
    col = lax.broadcasted_iota(jnp.int32, (tm, tm), 1)
    pos = tile_in_seq * tm + lax.broadcasted_iota(jnp.int32, (tm, 1), 0)
    has_prev = (tile_in_seq > 0).astype(F32)
    for g, w in enumerate(POOL_WINDOWS):
        sl = slice(g * POOL_GC, (g + 1) * POOL_GC)
        cur = cur_ref[:, sl]
        prev = prev_ref[:, sl]
        band_cur = ((row >= col) & (row - col < w)).astype(BF16)
        band_prev = (row + tm - col < w).astype(BF16)
        win = jnp.dot(band_cur, cur, preferred_element_type=F32)
        win = win + has_prev * jnp.dot(band_prev, prev, preferred_element_type=F32)
        cnt = jnp.minimum(pos + 1, w).astype(F32)
        pooled = (win / cnt - cur.astype(F32)).astype(BF16)
        y = jnp.dot(pooled, pw_ref[g], preferred_element_type=F32) * ps_ref[:, sl]
        o_ref[:, sl] = y.astype(o_ref.dtype)


def pool_mixer(proj, col_off, pool_w, pool_scale, seq, tm=256):
    t = proj.shape[0]
    width = len(POOL_WINDOWS) * POOL_GC
    cb = col_off // width
    tiles_per_seq = seq // tm
    return pl.pallas_call(
        functools.partial(_pool_kernel, tiles_per_seq=tiles_per_seq),
        grid=(t // tm,),
        in_specs=[pl.BlockSpec((tm, width), lambda i: (i, cb)),
                  pl.BlockSpec((tm, width), lambda i: (jnp.maximum(i - 1, 0), cb)),
                  pl.BlockSpec((len(POOL_WINDOWS), POOL_GC, POOL_GC), lambda i: (0, 0, 0)),
                  pl.BlockSpec((1, width), lambda i: (0, 0))],
        out_specs=pl.BlockSpec((tm, width), lambda i: (i, 0)),
        out_shape=jax.ShapeDtypeStruct((t, width), BF16),
        compiler_params=_params(1),
        name="pool_mixer",
    )(proj, proj, pool_w.astype(BF16), pool_scale.reshape(1, width))


def _swa_kernel(q_ref, kc_ref, kp_ref, vc_ref, vp_ref, sink_ref, o_ref, *, blocks_per_seq):
    blk = SWA_BLOCK
    nk = 2 * blk
    rep = SWA_HQ // SWA_HKV
    ppg = rep // 2
    i = pl.program_id(0)
    not_first = (i % blocks_per_seq) > 0
    lane = lax.broadcasted_iota(jnp.int32, (nk, LANES), 1)

    kk = jnp.concatenate([kp_ref[...], kc_ref[...]], axis=0).astype(F32)
    k0_lo = jnp.where(lane < HEAD_DIM, kk, 0.0)
    k1_hi = jnp.where(lane >= HEAD_DIM, kk, 0.0)
    kmats = ((k0_lo.astype(BF16), pltpu.roll(k0_lo, HEAD_DIM, 1).astype(BF16)),
             (pltpu.roll(k1_hi, HEAD_DIM, 1).astype(BF16), k1_hi.astype(BF16)))
    vt = jnp.concatenate([vp_ref[...], vc_ref[...]], axis=0).astype(F32).T
    ones_rows = (lax.broadcasted_iota(jnp.int32, (VT_ROWS - HEAD_DIM, nk), 0) == 0).astype(F32)
    vts = [jnp.concatenate([vt[g * HEAD_DIM:(g + 1) * HEAD_DIM], ones_rows], axis=0).astype(BF16)
           for g in range(SWA_HKV)]
    qg = [jnp.concatenate([q_ref[:, (g * ppg + pp) * LANES:(g * ppg + pp + 1) * LANES] for pp in range(ppg)], axis=0)
          for g in range(SWA_HKV)]

    ncol = ppg * blk
    key = lax.broadcasted_iota(jnp.int32, (nk, ncol), 0)
    col = lax.broadcasted_iota(jnp.int32, (nk, ncol), 1)
    dist = col % blk + blk - key
    band = (dist >= 0) & (dist < blk) & (not_first | (key >= blk))
    seg = lax.broadcasted_iota(jnp.int32, (1, ncol), 1) // blk

    tasks = [(g, par) for g in range(SWA_HKV) for par in range(2)]

    def scores(g, par):
        return lax.dot_general(kmats[g][par], qg[g], _NT, preferred_element_type=F32)

    outs = {}
    st_next = scores(*tasks[0])
    for n, (g, par) in enumerate(tasks):
        st = st_next
        if n + 1 < len(tasks):
            st_next = scores(*tasks[n + 1])
        sink = jnp.zeros((1, ncol), F32)
        for pp in range(ppg):
            sink = jnp.where(seg == pp, sink_ref[0, g * rep + 2 * pp + par], sink)
        sb = jnp.where(band, st, NEG_INF).astype(BF16)
        mb = jnp.maximum(jnp.max(sb, axis=0, keepdims=True), sink.astype(BF16))
        pm = jnp.exp(sb - mb)
        pv = jnp.dot(vts[g], pm, preferred_element_type=F32)
        den = pv[HEAD_DIM:HEAD_DIM + 1] + jnp.exp(sink - mb.astype(F32))
        outs[(g, par)] = pv[0:HEAD_DIM] / den
    for p in range(SWA_HQ // 2):
        g, pp = p // ppg, p % ppg
        pair_t = jnp.concatenate([outs[(g, 0)][:, pp * blk:(pp + 1) * blk], outs[(g, 1)][:, pp * blk:(pp + 1) * blk]],
                                 axis=0)
        o_ref[:, p * LANES:(p + 1) * LANES] = pair_t.T.astype(o_ref.dtype)


def swa_attention(q, k, proj, v_col, sinks, seq):
    t = q.shape[0]
    blk = SWA_BLOCK
    bps = seq // blk
    kvw = SWA_HKV * HEAD_DIM
    vb = v_col // kvw

    def prev(i):
        return jnp.maximum(i - 1, 0)

    return pl.pallas_call(
        functools.partial(_swa_kernel, blocks_per_seq=bps),
        grid=(t // blk,),
        in_specs=[pl.BlockSpec((blk, q.shape[1]), lambda i: (i, 0)),
                  pl.BlockSpec((blk, kvw), lambda i: (i, 0)),
                  pl.BlockSpec((blk, kvw), lambda i: (prev(i), 0)),
                  pl.BlockSpec((blk, kvw), lambda i: (i, vb)),
                  pl.BlockSpec((blk, kvw), lambda i: (prev(i), vb)),
                  pl.BlockSpec((1, SWA_HQ), lambda i: (0, 0), memory_space=pltpu.SMEM)],
        out_specs=pl.BlockSpec((blk, q.shape[1]), lambda i: (i, 0)),
        out_shape=jax.ShapeDtypeStruct(q.shape, BF16),
        compiler_params=_params(1),
        name="swa_attention",
    )(q, k, k, proj, proj, sinks.reshape(1, SWA_HQ).astype(F32))


VT_ROWS = HEAD_DIM + 16
MOBA_TASK_ROWS = 512


def _moba_kernel(q_ref, k_ref, v_ref, km_ref, o_ref, vt_ref, ke_ref, ko_ref, m_ref, acc_ref,
                 *, nblk, qt):
    blk = MOBA_BLOCK
    sub = qt // blk
    npair = q_ref.shape[1] // LANES
    qi = pl.program_id(2)
    lane = lax.broadcasted_iota(jnp.int32, (blk, LANES), 1)
    low = lane < HEAD_DIM

    @pl.when(qi == 0)
    def _():
        ones_row = (lax.broadcasted_iota(jnp.int32, (VT_ROWS - HEAD_DIM, vt_ref.shape[2]), 0) == 0).astype(BF16)
        for h in range(2 * npair):
            vt_ref[h, HEAD_DIM:VT_ROWS, :] = ones_row

        def fill(j, c):
            off = pl.multiple_of(j * blk, blk)
            vt = v_ref[pl.ds(off, blk), :].astype(F32).T.astype(BF16)
            for h in range(2 * npair):
                vt_ref[h, 0:HEAD_DIM, pl.ds(off, blk)] = vt[h * HEAD_DIM:(h + 1) * HEAD_DIM, :]
            for p in range(npair):
                kf = k_ref[pl.ds(off, blk), p * LANES:(p + 1) * LANES].astype(F32)
                ke_ref[p, pl.ds(off, blk), :] = jnp.where(low, kf, (lane - HEAD_DIM == j).astype(F32)).astype(BF16)
                ko_ref[p, pl.ds(off, blk), :] = jnp.where(low, (lane == j).astype(F32), kf).astype(BF16)
            return c
        lax.fori_loop(0, nblk, fill, 0)

    block_of_row = lax.broadcasted_iota(jnp.int32, (HEAD_DIM, blk), 0)

    def block_bias(gates_t, own):
        g = jnp.where(block_of_row < own, gates_t, -jnp.inf)
        sel = block_of_row == own
        for _ in range(MOBA_TOPK):
            mx = jnp.max(g, axis=0, keepdims=True)
            idx = jnp.min(jnp.where(g == mx, block_of_row, 1 << 20), axis=0, keepdims=True)
            hit = (block_of_row == idx) & (mx > -jnp.inf)
            sel = sel | hit
            g = jnp.where(hit, -jnp.inf, g)
        return jnp.where(sel, 0.0, NEG_INF)

    q_ext = []
    for p in range(npair):
        ext_e, ext_o = [], []
        for c in range(sub):
            own = qi * sub + c
            qp = q_ref[c * blk:(c + 1) * blk, p * LANES:(p + 1) * LANES]
            gates_t = lax.dot_general(km_ref[p], qp, _NT, preferred_element_type=F32)
            bias_t = jnp.concatenate([block_bias(gates_t[0:HEAD_DIM], own), block_bias(gates_t[HEAD_DIM:], own)],
                                     axis=0)
            bias = bias_t.T
            qf = qp.astype(F32)
            ext_e.append(jnp.where(low, qf, bias).astype(BF16))
            ext_o.append(jnp.where(low, bias, qf).astype(BF16))
        q_ext.append((jnp.concatenate(ext_e, axis=0), jnp.concatenate(ext_o, axis=0)))

    m_ref[...] = jnp.full(m_ref.shape, -jnp.inf, F32)
    acc_ref[...] = jnp.zeros(acc_ref.shape, F32)
    rows = sub * blk

    def steps(first_group, n_groups, trows, causal):
        start = first_group * rows
        tasks = [(pl.multiple_of(start + part * trows, trows), part, h)
                 for part in range(n_groups * rows // trows) for h in range(2 * npair)]

        def scores(off, part, h):
            k_ext = (ke_ref if h % 2 == 0 else ko_ref)[h // 2, pl.ds(off, trows), :]
            return lax.dot_general(k_ext, q_ext[h // 2][h % 2], _NT, preferred_element_type=F32)

        ahead = 1
        pending = [scores(*tasks[n]) for n in range(min(ahead, len(tasks)))]
        for n, (off, part, h) in enumerate(tasks):
            st = pending.pop(0)
            if n + ahead < len(tasks):
                pending.append(scores(*tasks[n + ahead]))
            if causal:
                key_idx = lax.broadcasted_iota(jnp.int32, (trows, qt), 0) + part * trows
                st = jnp.where(key_idx <= lax.broadcasted_iota(jnp.int32, (trows, qt), 1), st, NEG_INF)
            sb = st.astype(BF16)
            m = m_ref[h]
            m_new = jnp.maximum(m, jnp.max(sb, axis=0, keepdims=True).astype(F32))
            alpha = jnp.exp(m - m_new)
            pm = jnp.exp(sb - m_new.astype(BF16))
            pv = jnp.dot(vt_ref[h, :, pl.ds(off, trows)], pm, preferred_element_type=F32)
            acc_ref[h] = acc_ref[h] * alpha + pv
            m_ref[h] = m_new

    def body(u, c):
        steps(4 * u, 4, MOBA_TASK_ROWS, False)
        return c

    lax.fori_loop(0, qi // 4, body, 0)

    def tail(jt, c):
        steps(jt, 1, rows, False)
        return c

    lax.fori_loop((qi // 4) * 4, qi, tail, 0)

    steps(qi, 1, rows, True)
    outs = [acc_ref[h, 0:HEAD_DIM, :] / acc_ref[h, HEAD_DIM:HEAD_DIM + 1, :] for h in range(2 * npair)]
    o_ref[...] = jnp.concatenate(outs, axis=0).T.astype(o_ref.dtype)


def moba_attention(q, k, proj, v_col, gate_mats, batch, seq, qt=512, npair=2):
    t, width = q.shape
    nblk = seq // MOBA_BLOCK
    qt = min(qt, seq)
    gw = npair * LANES
    q3 = q.reshape(batch, seq, width)
    k3 = k.reshape(batch, seq, width)
    p3 = proj.reshape(batch, seq, proj.shape[1])
    vb = v_col // gw

    def resident(shape, index_map):
        return pl.BlockSpec(shape, index_map, pipeline_mode=pl.Buffered(1))

    out = pl.pallas_call(
        functools.partial(_moba_kernel, nblk=nblk, qt=qt),
        grid=(batch, width // gw, seq // qt),
        in_specs=[pl.BlockSpec((None, qt, gw), lambda b, g, i: (b, i, g)),
                  resident((None, seq, gw), lambda b, g, i: (b, 0, g)),
                  resident((None, seq, gw), lambda b, g, i: (b, 0, vb + g)),
                  pl.BlockSpec((None, npair, LANES, LANES), lambda b, g, i: (b, g, 0, 0))],
        out_specs=pl.BlockSpec((None, qt, gw), lambda b, g, i: (b, i, g)),
        out_shape=jax.ShapeDtypeStruct((batch, seq, width), BF16),
        scratch_shapes=[pltpu.VMEM((2 * npair, VT_ROWS, seq), BF16),
                        pltpu.VMEM((npair, seq, LANES), BF16), pltpu.VMEM((npair, seq, LANES), BF16),
                        pltpu.VMEM((2 * npair, 1, qt), F32),
                        pltpu.VMEM((2 * npair, VT_ROWS, qt), F32)],
        compiler_params=_params(3),
        name="moba_attention",
    )(q3, k3, p3, gate_mats)
    return out.reshape(t, width)


def moba_gate_mats(kmean, batch, nblk):
    km = kmean.reshape(batch, nblk, MOBA_HEADS // 2, 2, HEAD_DIM).astype(BF16)
    km = jnp.transpose(km, (0, 2, 3, 1, 4))
    pad_rows = LANES // 2 - nblk
    even = jnp.pad(km[:, :, 0], ((0, 0), (0, 0), (LANES // 2, pad_rows), (0, HEAD_DIM)))
    odd = jnp.pad(km[:, :, 1], ((0, 0), (0, 0), (0, LANES // 2 + pad_rows), (HEAD_DIM, 0)))
    return even + odd


def _outproj_kernel(x_ref, ya_ref, yb_ref, yc_ref, ga_ref, gb_ref, gc_ref,
                    wa_ref, wb_ref, wc_ref, wo_ref, gain_ref, xo_ref, h_ref):
    def gate(g_ref):
        return jax.nn.sigmoid(g_ref[...].astype(F32))

    merged = gate(ga_ref) * jnp.dot(ya_ref[...], wa_ref[...], preferred_element_type=F32)
    merged = merged + gate(gb_ref) * jnp.dot(yb_ref[...], wb_ref[...], preferred_element_type=F32)
    merged = merged + gate(gc_ref) * jnp.dot(yc_ref[...], wc_ref[...], preferred_element_type=F32)
    xn = x_ref[...] + jnp.dot(merged.astype(BF16), wo_ref[...], preferred_element_type=F32)
    xo_ref[...] = xn
    ms = jnp.mean(xn * xn, axis=-1, keepdims=True)
    h_ref[...] = xn * lax.rsqrt(ms + RMS_EPS) * gain_ref[...]


def out_proj(x2d, ya, yb, yc, proj, gate_col, wa, wb, wc, wo, gain, tm=256):
    t, d = x2d.shape
    gb0 = gate_col // d

    def rows(width):
        return pl.BlockSpec((tm, width), lambda i: (i, 0))

    def whole(w):
        return pl.BlockSpec(w.shape, lambda i: (0, 0), pipeline_mode=pl.Buffered(1))

    return pl.pallas_call(
        _outproj_kernel,
        grid=(t // tm,),
        in_specs=[rows(d), rows(ya.shape[1]), rows(yb.shape[1]), rows(yc.shape[1]),
                  pl.BlockSpec((tm, d), lambda i: (i, gb0)),
                  pl.BlockSpec((tm, d), lambda i: (i, gb0 + 1)),
                  pl.BlockSpec((tm, d), lambda i: (i, gb0 + 2)),
                  whole(wa), whole(wb), whole(wc), whole(wo),
                  pl.BlockSpec((1, d), lambda i: (0, 0))],
        out_specs=[rows(d), rows(d)],
        out_shape=[jax.ShapeDtypeStruct((t, d), F32), jax.ShapeDtypeStruct((t, d), F32)],
        compiler_params=_params(1),
        name="out_proj",
    )(x2d, ya, yb, yc, proj, proj, proj, wa, wb, wc, wo, gain.reshape(1, d))


def _router_kernel(h_ref, wr_ref, route_ref, count_ref, run_ref):
    tm = h_ref.shape[0]
    epg = EXPERTS_PER_GROUP

    @pl.when(pl.program_id(0) == 0)
    def _():
        run_ref[...] = jnp.zeros(run_ref.shape, F32)

    logits = lax.dot_general(wr_ref[...], h_ref[...].astype(BF16), _NT, preferred_element_type=F32)
    row8 = lax.broadcasted_iota(jnp.int32, (epg, tm), 0)
    glog = jnp.where(row8 < N_GROUPS, logits[N_EXPERTS:N_EXPERTS + epg, :], -jnp.inf)
    gmax = jnp.max(glog, axis=0, keepdims=True)
    gsel = jnp.min(jnp.where(glog == gmax, row8, epg), axis=0, keepdims=True)
    p_grp = 1.0 / jnp.sum(jnp.exp(glog - gmax), axis=0, keepdims=True)
    within = jnp.zeros((epg, tm), F32)
    for g in range(N_GROUPS):
        within = jnp.where(gsel == g, logits[g * epg:(g + 1) * epg, :], within)
    wmax = jnp.max(within, axis=0, keepdims=True)
    e = jnp.exp(within - wmax)
    prob = e / jnp.sum(e, axis=0, keepdims=True)
    p1 = jnp.max(prob, axis=0, keepdims=True)
    i1 = jnp.min(jnp.where(prob == p1, row8, epg), axis=0, keepdims=True)
    rest = jnp.where(row8 == i1, -jnp.inf, prob)
    p2 = jnp.max(rest, axis=0, keepdims=True)
    i2 = jnp.min(jnp.where(rest == p2, row8, epg), axis=0, keepdims=True)
    norm = p1 + p2
    w1 = p_grp * p1 / norm
    w2 = p_grp * p2 / norm
    e1 = gsel * epg + i1
    e2 = gsel * epg + i2

    row32 = lax.broadcasted_iota(jnp.int32, (N_EXPERTS, tm), 0)
    oh1 = (row32 == e1).astype(F32)
    oh2 = (row32 == e2).astype(F32)
    oh = oh1 + oh2
    before = (lax.broadcasted_iota(jnp.int32, (tm, tm), 0) < lax.broadcasted_iota(jnp.int32, (tm, tm), 1)).astype(BF16)
    prior = jnp.dot(oh.astype(BF16), before, preferred_element_type=F32) + run_ref[:, 0:1]
    r1 = jnp.sum(oh1 * prior, axis=0, keepdims=True)
    r2 = jnp.sum(oh2 * prior, axis=0, keepdims=True)
    run_new = run_ref[...] + jnp.sum(oh, axis=1, keepdims=True)
    run_ref[...] = run_new
    count_ref[...] = run_new
    zero = jnp.zeros((1, tm), F32)
    route_ref[...] = jnp.concatenate([e1.astype(F32), e2.astype(F32), r1, r2, w1, w2, zero, zero], axis=0)


def router(h, wr_t, tm=512):
    t, d = h.shape
    return pl.pallas_call(
        _router_kernel,
        grid=(t // tm,),
        in_specs=[pl.BlockSpec((tm, d), lambda i: (i, 0)), pl.BlockSpec(wr_t.shape, lambda i: (0, 0))],
        out_specs=[pl.BlockSpec((8, tm), lambda i: (0, i)), pl.BlockSpec((N_EXPERTS, LANES), lambda i: (0, 0))],
        out_shape=[jax.ShapeDtypeStruct((8, t), F32), jax.ShapeDtypeStruct((N_EXPERTS, LANES), F32)],
        scratch_shapes=[pltpu.VMEM((N_EXPERTS, LANES), F32)],
        compiler_params=_params(1),
        name="router",
    )(h, wr_t)


def _row_copy(src_ref, src_row, dst_ref, dst_row, sem):
    return pltpu.make_async_copy(src_ref.at[pl.ds(src_row, 1)], dst_ref.at[pl.ds(dst_row, 1)], sem)


def _dispatch_kernel(dest_ref, h_ref, init_ref, xs_ref, sem):
    del init_ref
    tm = h_ref.shape[0]

    def start(r, c):
        for k in range(2):
            _row_copy(h_ref, r, xs_ref, dest_ref[k, r], sem).start()
        return c

    def wait(r, c):
        for k in range(2):
            _row_copy(h_ref, r, xs_ref, dest_ref[k, r], sem).wait()
        return c

    lax.fori_loop(0, tm, start, 0, unroll=8)
    lax.fori_loop(0, tm, wait, 0, unroll=8)


def dispatch(h, dest, slots_init, tm=256):
    t, d = h.shape
    n_slots = slots_init.shape[0]
    dest3 = dest.reshape(2, t // tm, tm).transpose(1, 0, 2)
    return pl.pallas_call(
        _dispatch_kernel,
        grid=(t // tm,),
        in_specs=[pl.BlockSpec((None, 2, tm), lambda i: (i, 0, 0), memory_space=pltpu.SMEM),
                  pl.BlockSpec((tm, d), lambda i: (i, 0)),
                  pl.BlockSpec(memory_space=pl.ANY)],
        out_specs=pl.BlockSpec(memory_space=pl.ANY),
        out_shape=jax.ShapeDtypeStruct((n_slots, d), h.dtype),
        input_output_aliases={2: 0},
        scratch_shapes=[pltpu.SemaphoreType.DMA],
        compiler_params=_params(1),
        name="moe_dispatch",
    )(dest3, h, slots_init)


CAST_ROWS = 256


def _cast_rows(src_ref, dst_ref):
    def body(c, carry):
        sl = pl.ds(pl.multiple_of(c * CAST_ROWS, CAST_ROWS), CAST_ROWS)
        dst_ref[sl, :] = src_ref[sl, :].astype(dst_ref.dtype)
        return carry
    lax.fori_loop(0, src_ref.shape[0] // CAST_ROWS, body, 0)


def _expert_kernel(bexp_ref, nvalid_ref, next_ref, xs_ref, wg_hbm, wu_hbm, wd_hbm, ys_ref,
                   wg_st, wu_st, wd_st, wg_bf, wu_bf, wd_bf, state_ref, sems, *, layer):
    n = pl.program_id(0)
    nv = nvalid_ref[n]
    e = bexp_ref[n]

    def fetch(expert):
        return [pltpu.make_async_copy(src.at[layer, expert], dst, sems.at[k])
                for k, (src, dst) in enumerate(((wg_hbm, wg_st), (wu_hbm, wu_st), (wd_hbm, wd_st)))]

    @pl.when(n == 0)
    def _():
        state_ref[0] = -1
        state_ref[1] = -1

    @pl.when((nv > 0) & (state_ref[0] != e))
    def _():
        @pl.when(state_ref[1] != e)
        def _():
            for c in fetch(e):
                c.start()
            state_ref[1] = e

        for c in fetch(e):
            c.wait()
        _cast_rows(wg_st, wg_bf)
        _cast_rows(wu_st, wu_bf)
        _cast_rows(wd_st, wd_bf)
        state_ref[0] = e
        nxt = next_ref[n]

        @pl.when(nxt >= 0)
        def _():
            for c in fetch(nxt):
                c.start()
            state_ref[1] = nxt

    @pl.when(nv > 0)
    def _():
        row = lax.broadcasted_iota(jnp.int32, xs_ref.shape, 0)
        x = jnp.where(row < nv, xs_ref[...], 0.0).astype(BF16)
        g = jnp.dot(x, wg_bf[...], preferred_element_type=F32)
        u = jnp.dot(x, wu_bf[...], preferred_element_type=F32)
        act = (jax.nn.silu(g) * u).astype(BF16)
        ys_ref[...] = jnp.dot(act, wd_bf[...], preferred_element_type=F32)

    @pl.when(nv <= 0)
    def _():
        ys_ref[...] = jnp.zeros(ys_ref.shape, ys_ref.dtype)


def experts(xs, blk_exp, nvalid, next_exp, wg, wu, wd, layer):
    n_slots, d = xs.shape
    ff = wg.shape[3]
    nb = n_slots // MOE_BLOCK
    hbm = pl.BlockSpec(memory_space=pl.ANY)
    grid_spec = pltpu.PrefetchScalarGridSpec(
        num_scalar_prefetch=3,
        grid=(nb,),
        in_specs=[pl.BlockSpec((MOE_BLOCK, d), lambda n, be, nv, nx: (n, 0)), hbm, hbm, hbm],
        out_specs=pl.BlockSpec((MOE_BLOCK, d), lambda n, be, nv, nx: (n, 0)),
        scratch_shapes=[pltpu.VMEM((d, ff), F32), pltpu.VMEM((d, ff), F32), pltpu.VMEM((ff, d), F32),
                        pltpu.VMEM((d, ff), BF16), pltpu.VMEM((d, ff), BF16), pltpu.VMEM((ff, d), BF16),
                        pltpu.SMEM((2,), jnp.int32), pltpu.SemaphoreType.DMA((3,))],
    )
    return pl.pallas_call(
        functools.partial(_expert_kernel, layer=layer),
        grid_spec=grid_spec,
        out_shape=jax.ShapeDtypeStruct((n_slots, d), F32),
        compiler_params=_params(1),
        name="moe_experts",
    )(blk_exp, nvalid, next_exp, xs, wg, wu, wd)


def _combine_kernel(dest_ref, dest_next_ref, x_ref, w_ref, gain_ref, ys_ref, xo_ref, *rest):
    h_ref, buf_ref, sems = rest if len(rest) == 3 else (None,) + rest
    tm = x_ref.shape[0]
    i = pl.program_id(0)
    slot = i % 2

    def gather(d_ref, s, wait):
        def body(r, c):
            for k in range(2):
                cp = _row_copy(ys_ref, d_ref[k, r], buf_ref.at[s, k], r, sems.at[s])
                if wait:
                    cp.wait()
                else:
                    cp.start()
            return c
        lax.fori_loop(0, tm, body, 0, unroll=8)

    @pl.when(i == 0)
    def _():
        gather(dest_ref, 0, wait=False)

    @pl.when(i + 1 < pl.num_programs(0))
    def _():
        gather(dest_next_ref, 1 - slot, wait=False)

    gather(dest_ref, slot, wait=True)
    w = w_ref[...]
    xn = x_ref[...] + buf_ref[slot, 0] * w[:, 0:1] + buf_ref[slot, 1] * w[:, 1:2]
    xo_ref[...] = xn
    if h_ref is not None:
        ms = jnp.mean(xn * xn, axis=-1, keepdims=True)
        h_ref[...] = (xn * lax.rsqrt(ms + RMS_EPS) * gain_ref[...]).astype(h_ref.dtype)


def combine(x2d, ys, dest, wts, gain, emit_norm, tm=256):
    t, d = x2d.shape
    dest3 = dest.reshape(2, t // tm, tm).transpose(1, 0, 2)
    return pl.pallas_call(
        _combine_kernel,
        grid=(t // tm,),
        in_specs=[pl.BlockSpec((None, 2, tm), lambda i: (i, 0, 0), memory_space=pltpu.SMEM),
                  pl.BlockSpec((None, 2, tm), lambda i: (jnp.minimum(i + 1, t // tm - 1), 0, 0),
                               memory_space=pltpu.SMEM),
                  pl.BlockSpec((tm, d), lambda i: (i, 0)),
                  pl.BlockSpec((tm, 2), lambda i: (i, 0)),
                  pl.BlockSpec((1, d), lambda i: (0, 0)),
                  pl.BlockSpec(memory_space=pl.ANY)],
        out_specs=[pl.BlockSpec((tm, d), lambda i: (i, 0))] * (2 if emit_norm else 1),
        out_shape=[jax.ShapeDtypeStruct((t, d), F32)] + ([jax.ShapeDtypeStruct((t, d), BF16)] if emit_norm else []),
        scratch_shapes=[pltpu.VMEM((2, 2, tm, d), F32), pltpu.SemaphoreType.DMA((2,))],
        compiler_params=_params(1),
        name="moe_combine",
    )(dest3, dest3, x2d, wts, gain.reshape(1, d), ys)


def _rope_tables(seq):
    inv_freq = 1.0 / (ROPE_THETA ** (jnp.arange(0, HEAD_DIM, 2, dtype=F32) / HEAD_DIM))
    ang = jnp.arange(seq, dtype=F32)[:, None] * inv_freq[None, :]
    cos, sin = jnp.cos(ang), jnp.sin(ang)
    cos_t = jnp.concatenate([cos, cos, cos, cos], axis=1)
    sin_t = jnp.concatenate([-sin, sin, -sin, sin], axis=1)
    return cos_t, sin_t


def _moe_plan(route, counts, n_tokens):
    eid = route[0:2].astype(jnp.int32)
    rank = route[2:4].astype(jnp.int32)
    wts = route[4:6].T
    cnt = counts[:, 0].astype(jnp.int32)
    padded = ((cnt + MOE_BLOCK - 1) // MOE_BLOCK) * MOE_BLOCK
    p_ends = jnp.cumsum(padded)
    p_starts = p_ends - padded
    hit = eid[..., None] == jnp.arange(N_EXPERTS, dtype=jnp.int32)
    dest = rank + jnp.sum(jnp.where(hit, p_starts, 0), axis=-1)
    n_blocks = -(-(n_tokens * 2) // MOE_BLOCK) + N_EXPERTS
    blk_start = jnp.arange(n_blocks, dtype=jnp.int32) * MOE_BLOCK
    blk_exp = jnp.minimum(jnp.sum(p_ends[None, :] <= blk_start[:, None], axis=1), N_EXPERTS - 1).astype(jnp.int32)
    nvalid = jnp.clip(cnt[blk_exp] - (blk_start - p_starts[blk_exp]), 0, MOE_BLOCK).astype(jnp.int32)
    ids = jnp.arange(N_EXPERTS, dtype=jnp.int32)
    later_used = (ids[None, :] > ids[:, None]) & (cnt[None, :] > 0)
    next_used = jnp.min(jnp.where(later_used, ids[None, :], N_EXPERTS), axis=1)
    next_exp = jnp.where(next_used < N_EXPERTS, next_used, -1)[blk_exp].astype(jnp.int32)
    return dest, wts, blk_exp, nvalid, next_exp, n_blocks * MOE_BLOCK


def kernel(x, norm_mix, w_in, pool_w, pool_scale, swa_q_norm, swa_k_norm, swa_sinks, moba_q_norm, moba_k_norm,
           w_up_a, w_up_b, w_up_c, w_o, norm_ffn, w_router_group, w_router_expert, w_exp_gate, w_exp_up, w_exp_down):
    batch, seq, d = x.shape
    t = batch * seq
    depth = w_in.shape[0]
    pool_width = len(POOL_WINDOWS) * POOL_GC
    wqb, wkv, wm = SWA_HQ * HEAD_DIM, SWA_HKV * HEAD_DIM, MOBA_HEADS * HEAD_DIM

    names = ["xa", "qb", "kb", "vb", "qm", "km", "vm", "ga", "gb", "gc"]
    widths = [pool_width, wqb, wkv, wkv, wm, wm, wm, d, d, d]
    src, o = {}, 0
    for nme, w in zip(names, widths):
        src[nme] = (o, w)
        o += w
    order = ["ga", "gb", "gc", "qb", "qm", "km", "vm", "xa", "kb", "vb"]
    cols, o = {}, 0
    for nme in order:
        cols[nme] = o
        o += src[nme][1]
    tile_map = [0] * (o // IN_PROJ_TN)
    for nme in order:
        if nme == "vb":
            continue
        width = src[nme][1] + (src["vb"][1] if nme == "kb" else 0)
        for k in range(width // IN_PROJ_TN):
            tile_map[src[nme][0] // IN_PROJ_TN + k] = cols[nme] // IN_PROJ_TN + k
    tile_map = jnp.asarray(tile_map, jnp.int32)

    cos_t, sin_t = _rope_tables(seq)
    x2d = x.reshape(t, d)
    h = rmsnorm_cast(x2d, norm_mix[0])
    xs = None

    for l in range(depth):
        proj = in_proj(h, w_in, l, tile_map)

        def tile2(g):
            return jnp.concatenate([g, g], axis=0)

        gains = jnp.stack([tile2(swa_q_norm[l]), tile2(swa_k_norm[l]), tile2(moba_q_norm[l]), tile2(moba_k_norm[l])]
                          + [jnp.zeros((LANES,), F32)] * 4)
        qb, kb, qm, km, kmean = qk_prep(proj, cols, cos_t, sin_t, gains, seq)
        y_a = pool_mixer(proj, cols["xa"], pool_w[l], pool_scale[l], seq)
        y_b = swa_attention(qb, kb, proj, cols["vb"], swa_sinks[l], seq)
        gate_mats = moba_gate_mats(kmean, batch, seq // MOBA_BLOCK)
        y_c = moba_attention(qm, km, proj, cols["vm"], gate_mats, batch, seq)
        x2d, h_ffn = out_proj(x2d, y_a, y_b, y_c, proj, cols["ga"],
                              w_up_a[l].astype(BF16), w_up_b[l].astype(BF16), w_up_c[l].astype(BF16),
                              w_o[l].astype(BF16), norm_ffn[l])

        wr_t = jnp.concatenate([w_router_expert[l].T, w_router_group[l].T,
                                jnp.zeros((LANES - N_EXPERTS - N_GROUPS, d), F32)], axis=0).astype(BF16)
        route, counts = router(h_ffn, wr_t)
        dest, wts, blk_exp, nvalid, next_exp, n_slots = _moe_plan(route, counts, t)
        xs = dispatch(h_ffn, dest, jnp.zeros((n_slots, d), F32) if xs is None else xs)
        ys = experts(xs, blk_exp, nvalid, next_exp, w_exp_gate, w_exp_up, w_exp_down, l)
        last = l + 1 == depth
        outs = combine(x2d, ys, dest, wts, norm_mix[0 if last else l + 1], emit_norm=not last)
        x2d = outs[0]
        h = None if last else outs[1]
    return x2d.reshape(batch, seq, d)
```

```python
import functools

import jax
import jax.numpy as jnp
from jax import lax
from jax.experimental import pallas as pl
from jax.experimental.pallas import tpu as pltpu

F32 = jnp.float32
BF16 = jnp.bfloat16

HEAD_DIM = 64
ROPE_THETA = 10000.0
RMS_EPS = 1e-6
NEG_INF = -1e30

POOL_WINDOWS = (2, 4, 8, 16)
POOL_GC = 128

SWA_HQ = 16
SWA_HKV = 2
SWA_BLOCK = 128

MOBA_HEADS = 8
MOBA_BLOCK = 256
MOBA_TOPK = 3

N_GROUPS = 4
EXPERTS_PER_GROUP = 8
N_EXPERTS = 32
MOE_BLOCK = 256

LANES = 128
VMEM_LIMIT = 56 * 1024 * 1024

_NT = (((1,), (1,)), ((), ()))


def _params(n_axes, vmem=VMEM_LIMIT):
    return pltpu.CompilerParams(dimension_semantics=("arbitrary",) * n_axes, vmem_limit_bytes=vmem)


def _rmsnorm_kernel(x_ref, g_ref, o_ref):
    x = x_ref[...]
    ms = jnp.mean(x * x, axis=-1, keepdims=True)
    o_ref[...] = (x * lax.rsqrt(ms + RMS_EPS) * g_ref[...]).astype(o_ref.dtype)


def rmsnorm_cast(x2d, gain, tm=512):
    t, d = x2d.shape
    return pl.pallas_call(
        _rmsnorm_kernel,
        grid=(t // tm,),
        in_specs=[pl.BlockSpec((tm, d), lambda i: (i, 0)), pl.BlockSpec((1, d), lambda i: (0, 0))],
        out_specs=pl.BlockSpec((tm, d), lambda i: (i, 0)),
        out_shape=jax.ShapeDtypeStruct((t, d), BF16),
        compiler_params=_params(1),
        name="rmsnorm",
    )(x2d, gain.reshape(1, d))


IN_PROJ_TN = 256


def _in_proj_kernel(tile_map_ref, a_ref, w_ref, o_ref):
    del tile_map_ref
    o_ref[...] = jnp.dot(a_ref[...], w_ref[...].astype(BF16), preferred_element_type=F32).astype(o_ref.dtype)


def in_proj(h, w_in, layer, tile_map, tm=2048):
    t, k = h.shape
    n = w_in.shape[2]
    tn = IN_PROJ_TN
    tm = min(tm, t)
    grid_spec = pltpu.PrefetchScalarGridSpec(
        num_scalar_prefetch=1,
        grid=(t // tm, n // tn),
        in_specs=[pl.BlockSpec((tm, k), lambda i, j, tmap: (i, 0)),
                  pl.BlockSpec((None, k, tn), lambda i, j, tmap: (layer, 0, j))],
        out_specs=pl.BlockSpec((tm, tn), lambda i, j, tmap: (i, tmap[j])),
    )
    return pl.pallas_call(
        _in_proj_kernel,
        grid_spec=grid_spec,
        out_shape=jax.ShapeDtypeStruct((t, n), BF16),
        compiler_params=_params(2),
        name="in_proj",
    )(tile_map, h, w_in)


def _norm_rope(x, gain, cos, sin_signed, blockdiag, first_half):
    x2 = x * x
    hi = x2.astype(BF16)
    lo = (x2 - hi.astype(F32)).astype(BF16)
    ss = jnp.dot(hi, blockdiag, preferred_element_type=F32) + jnp.dot(lo, blockdiag, preferred_element_type=F32)
    y = x * lax.rsqrt(ss * (1.0 / HEAD_DIM) + RMS_EPS) * gain
    rot = jnp.where(first_half, pltpu.roll(y, LANES - HEAD_DIM // 2, 1), pltpu.roll(y, HEAD_DIM // 2, 1))
    return y * cos + rot * sin_signed


def _prep_kernel(qb_ref, kb_ref, qm_ref, km_ref, cos_ref, sin_ref, gains_ref,
                 qb_o, kb_o, qm_o, km_o, kmean_o, *, scale):
    tm = qb_ref.shape[0]
    cos = cos_ref[...]
    sin = sin_ref[...]
    lane = lax.broadcasted_iota(jnp.int32, (tm, LANES), 1)
    first_half = (lane % HEAD_DIM) < (HEAD_DIM // 2)
    r = lax.broadcasted_iota(jnp.int32, (LANES, LANES), 0) // HEAD_DIM
    c = lax.broadcasted_iota(jnp.int32, (LANES, LANES), 1) // HEAD_DIM
    blockdiag = (r == c).astype(BF16)

    def run(src, dst, gain_row, mult, n_groups):
        outs = []
        gain = gains_ref[gain_row:gain_row + 1, :]
        for g in range(n_groups):
            sl = slice(g * LANES, (g + 1) * LANES)
            y = _norm_rope(src[:, sl].astype(F32), gain, cos, sin, blockdiag, first_half)
            if mult != 1.0:
                y = y * mult
            dst[:, sl] = y.astype(dst.dtype)
            outs.append(y)
        return outs

    run(qb_ref, qb_o, 0, scale, qb_ref.shape[1] // LANES)
    run(kb_ref, kb_o, 1, 1.0, kb_ref.shape[1] // LANES)
    run(qm_ref, qm_o, 2, scale, qm_ref.shape[1] // LANES)
    kms = run(km_ref, km_o, 3, 1.0, km_ref.shape[1] // LANES)
    for g, y in enumerate(kms):
        for blk in range(tm // MOBA_BLOCK):
            s = jnp.sum(y[blk * MOBA_BLOCK:(blk + 1) * MOBA_BLOCK, :], axis=0, keepdims=True)
            kmean_o[blk, :, g * LANES:(g + 1) * LANES] = s * (1.0 / MOBA_BLOCK)


def qk_prep(proj, cols, cos_t, sin_t, gains, seq, tm=512):
    t = proj.shape[0]
    tiles_per_seq = seq // tm
    wqb, wkb, wqm, wkm = SWA_HQ * HEAD_DIM, SWA_HKV * HEAD_DIM, MOBA_HEADS * HEAD_DIM, MOBA_HEADS * HEAD_DIM

    def col_spec(width, off):
        return pl.BlockSpec((tm, width), lambda i, o=off // width: (i, o))

    tab_spec = pl.BlockSpec((tm, LANES), lambda i: (i % tiles_per_seq, 0))
    return pl.pallas_call(
        functools.partial(_prep_kernel, scale=HEAD_DIM ** -0.5),
        grid=(t // tm,),
        in_specs=[col_spec(wqb, cols["qb"]), col_spec(wkb, cols["kb"]), col_spec(wqm, cols["qm"]),
                  col_spec(wkm, cols["km"]), tab_spec, tab_spec, pl.BlockSpec((8, LANES), lambda i: (0, 0))],
        out_specs=[pl.BlockSpec((tm, wqb), lambda i: (i, 0)), pl.BlockSpec((tm, wkb), lambda i: (i, 0)),
                   pl.BlockSpec((tm, wqm), lambda i: (i, 0)), pl.BlockSpec((tm, wkm), lambda i: (i, 0)),
                   pl.BlockSpec((tm // MOBA_BLOCK, 1, wkm), lambda i: (i, 0, 0))],
        out_shape=[jax.ShapeDtypeStruct((t, wqb), BF16), jax.ShapeDtypeStruct((t, wkb), BF16),
                   jax.ShapeDtypeStruct((t, wqm), BF16), jax.ShapeDtypeStruct((t, wkm), BF16),
                   jax.ShapeDtypeStruct((t // MOBA_BLOCK, 1, wkm), F32)],
        compiler_params=_params(1),
        name="qk_prep",
    )(proj, proj, proj, proj, cos_t, sin_t, gains)


def _pool_kernel(cur_ref, prev_ref, pw_ref, ps_ref, o_ref, *, tiles_per_seq):
    tm = cur_ref.shape[0]
    i = pl.program_id(0)
    tile_in_seq = i % tiles_per_seq
    row = lax.broadcasted_iota(jnp.int32, (tm, tm), 0)
    col = lax.broadcasted_iota(jnp.int32, (tm, tm), 1)
    pos = tile_in_seq * tm + lax.broadcasted_iota(jnp.int32, (tm, 1), 0)
    has_prev = (tile_in_seq > 0).astype(F32)
    for g, w in enumerate(POOL_WINDOWS):
        sl = slice(g * POOL_GC, (g + 1) * POOL_GC)
        cur = cur_ref[:, sl]
        prev = prev_ref[:, sl]
        band_cur = ((row >= col) & (row - col < w)).astype(BF16)
        band_prev = (row + tm - col < w).astype(BF16)
        win = jnp.dot(band_cur, cur, preferred_element_type=F32)
        win = win + has_prev * jnp.dot(band_prev, prev, preferred_element_type=F32)
        cnt = jnp.minimum(pos + 1, w).astype(F32)
        pooled = (win / cnt - cur.astype(F32)).astype(BF16)
        y = jnp.dot(pooled, pw_ref[g], preferred_element_type=F32) * ps_ref[:, sl]
        o_ref[:, sl] = y.astype(o_ref.dtype)


def pool_mixer(proj, col_off, pool_w, pool_scale, seq, tm=256):
    t = proj.shape[0]
    width = len(POOL_WINDOWS) * POOL_GC
    cb = col_off // width
    tiles_per_seq = seq // tm
    return pl.pallas_call(
        functools.partial(_pool_kernel, tiles_per_seq=tiles_per_seq),
        grid=(t // tm,),
        in_specs=[pl.BlockSpec((tm, width), lambda i: (i, cb)),
                  pl.BlockSpec((tm, width), lambda i: (jnp.maximum(i - 1, 0), cb)),
                  pl.BlockSpec((len(POOL_WINDOWS), POOL_GC, POOL_GC), lambda i: (0, 0, 0)),
                  pl.BlockSpec((1, width), lambda i: (0, 0))],
        out_specs=pl.BlockSpec((tm, width), lambda i: (i, 0)),
        out_shape=jax.ShapeDtypeStruct((t, width), BF16),
        compiler_params=_params(1),
        name="pool_mixer",
    )(proj, proj, pool_w.astype(BF16), pool_scale.reshape(1, width))


def _swa_kernel(q_ref, kc_ref, kp_ref, vc_ref, vp_ref, sink_ref, o_ref, *, blocks_per_seq):
    blk = SWA_BLOCK
    nk = 2 * blk
    rep = SWA_HQ // SWA_HKV
    ppg = rep // 2
    i = pl.program_id(0)
    not_first = (i % blocks_per_seq) > 0
    lane = lax.broadcasted_iota(jnp.int32, (nk, LANES), 1)

    kk = jnp.concatenate([kp_ref[...], kc_ref[...]], axis=0).astype(F32)
    k0_lo = jnp.where(lane < HEAD_DIM, kk, 0.0)
    k1_hi = jnp.where(lane >= HEAD_DIM, kk, 0.0)
    kmats = ((k0_lo.astype(BF16), pltpu.roll(k0_lo, HEAD_DIM, 1).astype(BF16)),
             (pltpu.roll(k1_hi, HEAD_DIM, 1).astype(BF16), k1_hi.astype(BF16)))
    vt = jnp.concatenate([vp_ref[...], vc_ref[...]], axis=0).astype(F32).T
    ones_rows = (lax.broadcasted_iota(jnp.int32, (VT_ROWS - HEAD_DIM, nk), 0) == 0).astype(F32)
    vts = [jnp.concatenate([vt[g * HEAD_DIM:(g + 1) * HEAD_DIM], ones_rows], axis=0).astype(BF16)
           for g in range(SWA_HKV)]
    qg = [jnp.concatenate([q_ref[:, (g * ppg + pp) * LANES:(g * ppg + pp + 1) * LANES] for pp in range(ppg)], axis=0)
          for g in range(SWA_HKV)]

    ncol = ppg * blk
    key = lax.broadcasted_iota(jnp.int32, (nk, ncol), 0)
    col = lax.broadcasted_iota(jnp.int32, (nk, ncol), 1)
    dist = col % blk + blk - key
    band = (dist >= 0) & (dist < blk) & (not_first | (key >= blk))
    seg = lax.broadcasted_iota(jnp.int32, (1, ncol), 1) // blk

    tasks = [(g, par) for g in range(SWA_HKV) for par in range(2)]

    def scores(g, par):
        return lax.dot_general(kmats[g][par], qg[g], _NT, preferred_element_type=F32)

    outs = {}
    st_next = scores(*tasks[0])
    for n, (g, par) in enumerate(tasks):
        st = st_next
        if n + 1 < len(tasks):
            st_next = scores(*tasks[n + 1])
        sink = jnp.zeros((1, ncol), F32)
        for pp in range(ppg):
            sink = jnp.where(seg == pp, sink_ref[0, g * rep + 2 * pp + par], sink)
        sb = jnp.where(band, st, NEG_INF).astype(BF16)
        mb = jnp.maximum(jnp.max(sb, axis=0, keepdims=True), sink.astype(BF16))
        pm = jnp.exp(sb - mb)
        pv = jnp.dot(vts[g], pm, preferred_element_type=F32)
        den = pv[HEAD_DIM:HEAD_DIM + 1] + jnp.exp(sink - mb.astype(F32))
        outs[(g, par)] = pv[0:HEAD_DIM] / den
    for p in range(SWA_HQ // 2):
        g, pp = p // ppg, p % ppg
        pair_t = jnp.concatenate([outs[(g, 0)][:, pp * blk:(pp + 1) * blk], outs[(g, 1)][:, pp * blk:(pp + 1) * blk]],
                                 axis=0)
        o_ref[:, p * LANES:(p + 1) * LANES] = pair_t.T.astype(o_ref.dtype)


def swa_attention(q, k, proj, v_col, sinks, seq):
    t = q.shape[0]
    blk = SWA_BLOCK
    bps = seq // blk
    kvw = SWA_HKV * HEAD_DIM
    vb = v_col // kvw

    def prev(i):
        return jnp.maximum(i - 1, 0)

    return pl.pallas_call(
        functools.partial(_swa_kernel, blocks_per_seq=bps),
        grid=(t // blk,),
        in_specs=[pl.BlockSpec((blk, q.shape[1]), lambda i: (i, 0)),
                  pl.BlockSpec((blk, kvw), lambda i: (i, 0)),
                  pl.BlockSpec((blk, kvw), lambda i: (prev(i), 0)),
                  pl.BlockSpec((blk, kvw), lambda i: (i, vb)),
                  pl.BlockSpec((blk, kvw), lambda i: (prev(i), vb)),
                  pl.BlockSpec((1, SWA_HQ), lambda i: (0, 0), memory_space=pltpu.SMEM)],
        out_specs=pl.BlockSpec((blk, q.shape[1]), lambda i: (i, 0)),
        out_shape=jax.ShapeDtypeStruct(q.shape, BF16),
        compiler_params=_params(1),
        name="swa_attention",
    )(q, k, k, proj, proj, sinks.reshape(1, SWA_HQ).astype(F32))


VT_ROWS = HEAD_DIM + 16
MOBA_TASK_ROWS = 512


def _moba_kernel(q_ref, k_ref, v_ref, km_ref, o_ref, vt_ref, ke_ref, ko_ref, m_ref, acc_ref,
                 *, nblk, qt):
    blk = MOBA_BLOCK
    sub = qt // blk
    npair = q_ref.shape[1] // LANES
    qi = pl.program_id(2)
    lane = lax.broadcasted_iota(jnp.int32, (blk, LANES), 1)
    low = lane < HEAD_DIM

    @pl.when(qi == 0)
    def _():
        ones_row = (lax.broadcasted_iota(jnp.int32, (VT_ROWS - HEAD_DIM, vt_ref.shape[2]), 0) == 0).astype(BF16)
        for h in range(2 * npair):
            vt_ref[h, HEAD_DIM:VT_ROWS, :] = ones_row

        def fill(j, c):
            off = pl.multiple_of(j * blk, blk)
            vt = v_ref[pl.ds(off, blk), :].astype(F32).T.astype(BF16)
            for h in range(2 * npair):
                vt_ref[h, 0:HEAD_DIM, pl.ds(off, blk)] = vt[h * HEAD_DIM:(h + 1) * HEAD_DIM, :]
            for p in range(npair):
                kf = k_ref[pl.ds(off, blk), p * LANES:(p + 1) * LANES].astype(F32)
                ke_ref[p, pl.ds(off, blk), :] = jnp.where(low, kf, (lane - HEAD_DIM == j).astype(F32)).astype(BF16)
                ko_ref[p, pl.ds(off, blk), :] = jnp.where(low, (lane == j).astype(F32), kf).astype(BF16)
            return c
        lax.fori_loop(0, nblk, fill, 0)

    ncol = npair * qt
    block_of_row = lax.broadcasted_iota(jnp.int32, (HEAD_DIM, ncol), 0)
    own = qi * sub + (lax.broadcasted_iota(jnp.int32, (1, ncol), 1) % qt) // blk

    def block_bias(gates_t):
        g = jnp.where(block_of_row < own, gates_t, -jnp.inf)
        sel = block_of_row == own
        for _ in range(MOBA_TOPK):
            mx = jnp.max(g, axis=0, keepdims=True)
            idx = jnp.min(jnp.where(g == mx, block_of_row, 1 << 20), axis=0, keepdims=True)
            hit = (block_of_row == idx) & (mx > -jnp.inf)
            sel = sel | hit
            g = jnp.where(hit, -jnp.inf, g)
        return jnp.where(sel, 0.0, NEG_INF)

    gates_t = jnp.concatenate(
        [lax.dot_general(km_ref[p], q_ref[:, p * LANES:(p + 1) * LANES], _NT, preferred_element_type=F32)
         for p in range(npair)], axis=1)
    bias_t = jnp.concatenate([block_bias(gates_t[0:HEAD_DIM]), block_bias(gates_t[HEAD_DIM:])], axis=0)
    lane_q = lax.broadcasted_iota(jnp.int32, (qt, LANES), 1)
    low_q = lane_q < HEAD_DIM
    q_ext = []
    for p in range(npair):
        bias = bias_t[:, p * qt:(p + 1) * qt].T
        qf = q_ref[:, p * LANES:(p + 1) * LANES].astype(F32)
        q_ext.append((jnp.where(low_q, qf, bias).astype(BF16),
                      jnp.where(low_q, bias, qf).astype(BF16)))

    m_ref[...] = jnp.full(m_ref.shape, -jnp.inf, F32)
    acc_ref[...] = jnp.zeros(acc_ref.shape, F32)
    rows = sub * blk

    def steps(first_group, n_groups, trows, causal):
        start = first_group * rows
        tasks = [(pl.multiple_of(start + part * trows, trows), part, h)
                 for part in range(n_groups * rows // trows) for h in range(2 * npair)]

        def scores(off, part, h):
            k_ext = (ke_ref if h % 2 == 0 else ko_ref)[h // 2, pl.ds(off, trows), :]
            return lax.dot_general(k_ext, q_ext[h // 2][h % 2], _NT, preferred_element_type=F32)

        ahead = 1
        pending = [scores(*tasks[n]) for n in range(min(ahead, len(tasks)))]
        for n, (off, part, h) in enumerate(tasks):
            st = pending.pop(0)
            if n + ahead < len(tasks):
                pending.append(scores(*tasks[n + ahead]))
            if causal:
                key_idx = lax.broadcasted_iota(jnp.int32, (trows, qt), 0) + part * trows
                st = jnp.where(key_idx <= lax.broadcasted_iota(jnp.int32, (trows, qt), 1), st, NEG_INF)
            sb = st.astype(BF16)
            m = m_ref[h]
            m_new = jnp.maximum(m, jnp.max(sb, axis=0, keepdims=True).astype(F32))
            alpha = jnp.exp(m - m_new)
            pm = jnp.exp(sb - m_new.astype(BF16))
            pv = jnp.dot(vt_ref[h, :, pl.ds(off, trows)], pm, preferred_element_type=F32)
            acc_ref[h] = acc_ref[h] * alpha + pv
            m_ref[h] = m_new

    def body(u, c):
        steps(4 * u, 4, MOBA_TASK_ROWS, False)
        return c

    lax.fori_loop(0, qi // 4, body, 0)

    def tail(jt, c):
        steps(jt, 1, rows, False)
        return c

    lax.fori_loop((qi // 4) * 4, qi, tail, 0)

    steps(qi, 1, rows, True)
    outs = [acc_ref[h, 0:HEAD_DIM, :] / acc_ref[h, HEAD_DIM:HEAD_DIM + 1, :] for h in range(2 * npair)]
    o_ref[...] = jnp.concatenate(outs, axis=0).T.astype(o_ref.dtype)


def moba_attention(q, k, proj, v_col, gate_mats, batch, seq, qt=512, npair=2):
    t, width = q.shape
    nblk = seq // MOBA_BLOCK
    qt = min(qt, seq)
    gw = npair * LANES
    q3 = q.reshape(batch, seq, width)
    k3 = k.reshape(batch, seq, width)
    p3 = proj.reshape(batch, seq, proj.shape[1])
    vb = v_col // gw

    def resident(shape, index_map):
        return pl.BlockSpec(shape, index_map, pipeline_mode=pl.Buffered(1))

    out = pl.pallas_call(
        functools.partial(_moba_kernel, nblk=nblk, qt=qt),
        grid=(batch, width // gw, seq // qt),
        in_specs=[pl.BlockSpec((None, qt, gw), lambda b, g, i: (b, i, g)),
                  resident((None, seq, gw), lambda b, g, i: (b, 0, g)),
                  resident((None, seq, gw), lambda b, g, i: (b, 0, vb + g)),
                  pl.BlockSpec((None, npair, LANES, LANES), lambda b, g, i: (b, g, 0, 0))],
        out_specs=pl.BlockSpec((None, qt, gw), lambda b, g, i: (b, i, g)),
        out_shape=jax.ShapeDtypeStruct((batch, seq, width), BF16),
        scratch_shapes=[pltpu.VMEM((2 * npair, VT_ROWS, seq), BF16),
                        pltpu.VMEM((npair, seq, LANES), BF16), pltpu.VMEM((npair, seq, LANES), BF16),
                        pltpu.VMEM((2 * npair, 1, qt), F32),
                        pltpu.VMEM((2 * npair, VT_ROWS, qt), F32)],
        compiler_params=_params(3),
        name="moba_attention",
    )(q3, k3, p3, gate_mats)
    return out.reshape(t, width)


def moba_gate_mats(kmean, batch, nblk):
    km = kmean.reshape(batch, nblk, MOBA_HEADS // 2, 2, HEAD_DIM).astype(BF16)
    km = jnp.transpose(km, (0, 2, 3, 1, 4))
    pad_rows = LANES // 2 - nblk
    even = jnp.pad(km[:, :, 0], ((0, 0), (0, 0), (LANES // 2, pad_rows), (0, HEAD_DIM)))
    odd = jnp.pad(km[:, :, 1], ((0, 0), (0, 0), (0, LANES // 2 + pad_rows), (HEAD_DIM, 0)))
    return even + odd


def _outproj_kernel(x_ref, ya_ref, yb_ref, yc_ref, ga_ref, gb_ref, gc_ref,
                    wa_ref, wb_ref, wc_ref, wo_ref, gain_ref, xo_ref, h_ref):
    def gate(g_ref):
        return jax.nn.sigmoid(g_ref[...].astype(F32))

    merged = gate(ga_ref) * jnp.dot(ya_ref[...], wa_ref[...], preferred_element_type=F32)
    merged = merged + gate(gb_ref) * jnp.dot(yb_ref[...], wb_ref[...], preferred_element_type=F32)
    merged = merged + gate(gc_ref) * jnp.dot(yc_ref[...], wc_ref[...], preferred_element_type=F32)
    xn = x_ref[...] + jnp.dot(merged.astype(BF16), wo_ref[...], preferred_element_type=F32)
    xo_ref[...] = xn
    ms = jnp.mean(xn * xn, axis=-1, keepdims=True)
    h_ref[...] = xn * lax.rsqrt(ms + RMS_EPS) * gain_ref[...]


def out_proj(x2d, ya, yb, yc, proj, gate_col, wa, wb, wc, wo, gain, tm=256):
    t, d = x2d.shape
    gb0 = gate_col // d

    def rows(width):
        return pl.BlockSpec((tm, width), lambda i: (i, 0))

    def whole(w):
        return pl.BlockSpec(w.shape, lambda i: (0, 0), pipeline_mode=pl.Buffered(1))

    return pl.pallas_call(
        _outproj_kernel,
        grid=(t // tm,),
        in_specs=[rows(d), rows(ya.shape[1]), rows(yb.shape[1]), rows(yc.shape[1]),
                  pl.BlockSpec((tm, d), lambda i: (i, gb0)),
                  pl.BlockSpec((tm, d), lambda i: (i, gb0 + 1)),
                  pl.BlockSpec((tm, d), lambda i: (i, gb0 + 2)),
                  whole(wa), whole(wb), whole(wc), whole(wo),
                  pl.BlockSpec((1, d), lambda i: (0, 0))],
        out_specs=[rows(d), rows(d)],
        out_shape=[jax.ShapeDtypeStruct((t, d), F32), jax.ShapeDtypeStruct((t, d), F32)],
        compiler_params=_params(1),
        name="out_proj",
    )(x2d, ya, yb, yc, proj, proj, proj, wa, wb, wc, wo, gain.reshape(1, d))


def _router_kernel(h_ref, wr_ref, route_ref, count_ref, run_ref):
    tm = h_ref.shape[0]
    epg = EXPERTS_PER_GROUP

    @pl.when(pl.program_id(0) == 0)
    def _():
        run_ref[...] = jnp.zeros(run_ref.shape, F32)

    logits = lax.dot_general(wr_ref[...], h_ref[...].astype(BF16), _NT, preferred_element_type=F32)
    row8 = lax.broadcasted_iota(jnp.int32, (epg, tm), 0)
    glog = jnp.where(row8 < N_GROUPS, logits[N_EXPERTS:N_EXPERTS + epg, :], -jnp.inf)
    gmax = jnp.max(glog, axis=0, keepdims=True)
    gsel = jnp.min(jnp.where(glog == gmax, row8, epg), axis=0, keepdims=True)
    p_grp = 1.0 / jnp.sum(jnp.exp(glog - gmax), axis=0, keepdims=True)
    within = jnp.zeros((epg, tm), F32)
    for g in range(N_GROUPS):
        within = jnp.where(gsel == g, logits[g * epg:(g + 1) * epg, :], within)
    wmax = jnp.max(within, axis=0, keepdims=True)
    e = jnp.exp(within - wmax)
    prob = e / jnp.sum(e, axis=0, keepdims=True)
    p1 = jnp.max(prob, axis=0, keepdims=True)
    i1 = jnp.min(jnp.where(prob == p1, row8, epg), axis=0, keepdims=True)
    rest = jnp.where(row8 == i1, -jnp.inf, prob)
    p2 = jnp.max(rest, axis=0, keepdims=True)
    i2 = jnp.min(jnp.where(rest == p2, row8, epg), axis=0, keepdims=True)
    norm = p1 + p2
    w1 = p_grp * p1 / norm
    w2 = p_grp * p2 / norm
    e1 = gsel * epg + i1
    e2 = gsel * epg + i2

    row32 = lax.broadcasted_iota(jnp.int32, (N_EXPERTS, tm), 0)
    oh1 = (row32 == e1).astype(F32)
    oh2 = (row32 == e2).astype(F32)
    oh = oh1 + oh2
    before = (lax.broadcasted_iota(jnp.int32, (tm, tm), 0) < lax.broadcasted_iota(jnp.int32, (tm, tm), 1)).astype(BF16)
    prior = jnp.dot(oh.astype(BF16), before, preferred_element_type=F32) + run_ref[:, 0:1]
    r1 = jnp.sum(oh1 * prior, axis=0, keepdims=True)
    r2 = jnp.sum(oh2 * prior, axis=0, keepdims=True)
    run_new = run_ref[...] + jnp.sum(oh, axis=1, keepdims=True)
    run_ref[...] = run_new
    count_ref[...] = run_new
    zero = jnp.zeros((1, tm), F32)
    route_ref[...] = jnp.concatenate([e1.astype(F32), e2.astype(F32), r1, r2, w1, w2, zero, zero], axis=0)


def router(h, wr_t, tm=512):
    t, d = h.shape
    return pl.pallas_call(
        _router_kernel,
        grid=(t // tm,),
        in_specs=[pl.BlockSpec((tm, d), lambda i: (i, 0)), pl.BlockSpec(wr_t.shape, lambda i: (0, 0))],
        out_specs=[pl.BlockSpec((8, tm), lambda i: (0, i)), pl.BlockSpec((N_EXPERTS, LANES), lambda i: (0, 0))],
        out_shape=[jax.ShapeDtypeStruct((8, t), F32), jax.ShapeDtypeStruct((N_EXPERTS, LANES), F32)],
        scratch_shapes=[pltpu.VMEM((N_EXPERTS, LANES), F32)],
        compiler_params=_params(1),
        name="router",
    )(h, wr_t)


def _row_copy(src_ref, src_row, dst_ref, dst_row, sem):
    return pltpu.make_async_copy(src_ref.at[pl.ds(src_row, 1)], dst_ref.at[pl.ds(dst_row, 1)], sem)


def _dispatch_kernel(dest_ref, h_ref, init_ref, xs_ref, sem):
    del init_ref
    tm = h_ref.shape[0]

    def start(r, c):
        for k in range(2):
            _row_copy(h_ref, r, xs_ref, dest_ref[k, r], sem).start()
        return c

    def wait(r, c):
        for k in range(2):
            _row_copy(h_ref, r, xs_ref, dest_ref[k, r], sem).wait()
        return c

    for r in range(tm):
        start(r, 0)
    lax.fori_loop(0, tm, wait, 0, unroll=8)


def dispatch(h, dest, slots_init, tm=256):
    t, d = h.shape
    n_slots = slots_init.shape[0]
    dest3 = dest.reshape(2, t // tm, tm).transpose(1, 0, 2)
    return pl.pallas_call(
        _dispatch_kernel,
        grid=(t // tm,),
        in_specs=[pl.BlockSpec((None, 2, tm), lambda i: (i, 0, 0), memory_space=pltpu.SMEM),
                  pl.BlockSpec((tm, d), lambda i: (i, 0)),
                  pl.BlockSpec(memory_space=pl.ANY)],
        out_specs=pl.BlockSpec(memory_space=pl.ANY),
        out_shape=jax.ShapeDtypeStruct((n_slots, d), h.dtype),
        input_output_aliases={2: 0},
        scratch_shapes=[pltpu.SemaphoreType.DMA],
        compiler_params=_params(1),
        name="moe_dispatch",
    )(dest3, h, slots_init)


CAST_ROWS = 256


def _cast_rows(src_ref, dst_ref):
    def body(c, carry):
        sl = pl.ds(pl.multiple_of(c * CAST_ROWS, CAST_ROWS), CAST_ROWS)
        dst_ref[sl, :] = src_ref[sl, :].astype(dst_ref.dtype)
        return carry
    lax.fori_loop(0, src_ref.shape[0] // CAST_ROWS, body, 0)


def _expert_kernel(bexp_ref, nvalid_ref, next_ref, xs_ref, wg_hbm, wu_hbm, wd_hbm, ys_ref,
                   wg_st, wu_st, wd_st, wg_bf, wu_bf, wd_bf, state_ref, sems, *, layer):
    n = pl.program_id(0)
    nv = nvalid_ref[n]
    e = bexp_ref[n]

    def fetch(expert):
        return [pltpu.make_async_copy(src.at[layer, expert], dst, sems.at[k])
                for k, (src, dst) in enumerate(((wg_hbm, wg_st), (wu_hbm, wu_st), (wd_hbm, wd_st)))]

    @pl.when(n == 0)
    def _():
        state_ref[0] = -1
        state_ref[1] = -1

    @pl.when((nv > 0) & (state_ref[0] != e))
    def _():
        @pl.when(state_ref[1] != e)
        def _():
            for c in fetch(e):
                c.start()
            state_ref[1] = e

        for c in fetch(e):
            c.wait()
        _cast_rows(wg_st, wg_bf)
        _cast_rows(wu_st, wu_bf)
        _cast_rows(wd_st, wd_bf)
        state_ref[0] = e
        nxt = next_ref[n]

        @pl.when(nxt >= 0)
        def _():
            for c in fetch(nxt):
                c.start()
            state_ref[1] = nxt

    @pl.when(nv > 0)
    def _():
        row = lax.broadcasted_iota(jnp.int32, xs_ref.shape, 0)
        x = jnp.where(row < nv, xs_ref[...], 0.0).astype(BF16)
        g = jnp.dot(x, wg_bf[...], preferred_element_type=F32)
        u = jnp.dot(x, wu_bf[...], preferred_element_type=F32)
        act = (jax.nn.silu(g) * u).astype(BF16)
        ys_ref[...] = jnp.dot(act, wd_bf[...], preferred_element_type=F32)

    @pl.when(nv <= 0)
    def _():
        ys_ref[...] = jnp.zeros(ys_ref.shape, ys_ref.dtype)


def experts(xs, blk_exp, nvalid, next_exp, wg, wu, wd, layer):
    n_slots, d = xs.shape
    ff = wg.shape[3]
    nb = n_slots // MOE_BLOCK
    hbm = pl.BlockSpec(memory_space=pl.ANY)
    grid_spec = pltpu.PrefetchScalarGridSpec(
        num_scalar_prefetch=3,
        grid=(nb,),
        in_specs=[pl.BlockSpec((MOE_BLOCK, d), lambda n, be, nv, nx: (n, 0)), hbm, hbm, hbm],
        out_specs=pl.BlockSpec((MOE_BLOCK, d), lambda n, be, nv, nx: (n, 0)),
        scratch_shapes=[pltpu.VMEM((d, ff), F32), pltpu.VMEM((d, ff), F32), pltpu.VMEM((ff, d), F32),
                        pltpu.VMEM((d, ff), BF16), pltpu.VMEM((d, ff), BF16), pltpu.VMEM((ff, d), BF16),
                        pltpu.SMEM((2,), jnp.int32), pltpu.SemaphoreType.DMA((3,))],
    )
    return pl.pallas_call(
        functools.partial(_expert_kernel, layer=layer),
        grid_spec=grid_spec,
        out_shape=jax.ShapeDtypeStruct((n_slots, d), F32),
        compiler_params=_params(1),
        name="moe_experts",
    )(blk_exp, nvalid, next_exp, xs, wg, wu, wd)


def _combine_kernel(dest_ref, dest_next_ref, x_ref, w_ref, gain_ref, ys_ref, xo_ref, *rest):
    h_ref, buf_ref, sems = rest if len(rest) == 3 else (None,) + rest
    tm = x_ref.shape[0]
    i = pl.program_id(0)
    slot = i % 2

    def gather(d_ref, s, wait):
        def body(r, c):
            for k in range(2):
                cp = _row_copy(ys_ref, d_ref[k, r], buf_ref.at[s, k], r, sems.at[s])
                if wait:
                    cp.wait()
                else:
                    cp.start()
            return c
        if wait:
            lax.fori_loop(0, tm, body, 0, unroll=8)
        else:
            for r in range(tm):
                body(r, 0)

    @pl.when(i == 0)
    def _():
        gather(dest_ref, 0, wait=False)

    @pl.when(i + 1 < pl.num_programs(0))
    def _():
        gather(dest_next_ref, 1 - slot, wait=False)

    gather(dest_ref, slot, wait=True)
    w = w_ref[...]
    xn = x_ref[...] + buf_ref[slot, 0] * w[:, 0:1] + buf_ref[slot, 1] * w[:, 1:2]
    xo_ref[...] = xn
    if h_ref is not None:
        ms = jnp.mean(xn * xn, axis=-1, keepdims=True)
        h_ref[...] = (xn * lax.rsqrt(ms + RMS_EPS) * gain_ref[...]).astype(h_ref.dtype)


def combine(x2d, ys, dest, wts, gain, emit_norm, tm=256):
    t, d = x2d.shape
    dest3 = dest.reshape(2, t // tm, tm).transpose(1, 0, 2)
    return pl.pallas_call(
        _combine_kernel,
        grid=(t // tm,),
        in_specs=[pl.BlockSpec((None, 2, tm), lambda i: (i, 0, 0), memory_space=pltpu.SMEM),
                  pl.BlockSpec((None, 2, tm), lambda i: (jnp.minimum(i + 1, t // tm - 1), 0, 0),
                               memory_space=pltpu.SMEM),
                  pl.BlockSpec((tm, d), lambda i: (i, 0)),
                  pl.BlockSpec((tm, 2), lambda i: (i, 0)),
                  pl.BlockSpec((1, d), lambda i: (0, 0)),
                  pl.BlockSpec(memory_space=pl.ANY)],
        out_specs=[pl.BlockSpec((tm, d), lambda i: (i, 0))] * (2 if emit_norm else 1),
        out_shape=[jax.ShapeDtypeStruct((t, d), F32)] + ([jax.ShapeDtypeStruct((t, d), BF16)] if emit_norm else []),
        scratch_shapes=[pltpu.VMEM((2, 2, tm, d), F32), pltpu.SemaphoreType.DMA((2,))],
        compiler_params=_params(1),
        name="moe_combine",
    )(dest3, dest3, x2d, wts, gain.reshape(1, d), ys)


def _rope_tables(seq):
    inv_freq = 1.0 / (ROPE_THETA ** (jnp.arange(0, HEAD_DIM, 2, dtype=F32) / HEAD_DIM))
    ang = jnp.arange(seq, dtype=F32)[:, None] * inv_freq[None, :]
    cos, sin = jnp.cos(ang), jnp.sin(ang)
    cos_t = jnp.concatenate([cos, cos, cos, cos], axis=1)
    sin_t = jnp.concatenate([-sin, sin, -sin, sin], axis=1)
    return cos_t, sin_t


def _moe_plan(route, counts, n_tokens):
    eid = route[0:2].astype(jnp.int32)
    rank = route[2:4].astype(jnp.int32)
    wts = route[4:6].T
    cnt = counts[:, 0].astype(jnp.int32)
    padded = ((cnt + MOE_BLOCK - 1) // MOE_BLOCK) * MOE_BLOCK
    p_ends = jnp.cumsum(padded)
    p_starts = p_ends - padded
    hit = eid[..., None] == jnp.arange(N_EXPERTS, dtype=jnp.int32)
    dest = rank + jnp.sum(jnp.where(hit, p_starts, 0), axis=-1)
    n_blocks = -(-(n_tokens * 2) // MOE_BLOCK) + N_EXPERTS
    blk_start = jnp.arange(n_blocks, dtype=jnp.int32) * MOE_BLOCK
    blk_exp = jnp.minimum(jnp.sum(p_ends[None, :] <= blk_start[:, None], axis=1), N_EXPERTS - 1).astype(jnp.int32)
    nvalid = jnp.clip(cnt[blk_exp] - (blk_start - p_starts[blk_exp]), 0, MOE_BLOCK).astype(jnp.int32)
    ids = jnp.arange(N_EXPERTS, dtype=jnp.int32)
    later_used = (ids[None, :] > ids[:, None]) & (cnt[None, :] > 0)
    next_used = jnp.min(jnp.where(later_used, ids[None, :], N_EXPERTS), axis=1)
    next_exp = jnp.where(next_used < N_EXPERTS, next_used, -1)[blk_exp].astype(jnp.int32)
    return dest, wts, blk_exp, nvalid, next_exp, n_blocks * MOE_BLOCK


def kernel(x, norm_mix, w_in, pool_w, pool_scale, swa_q_norm, swa_k_norm, swa_sinks, moba_q_norm, moba_k_norm,
           w_up_a, w_up_b, w_up_c, w_o, norm_ffn, w_router_group, w_router_expert, w_exp_gate, w_exp_up, w_exp_down):
    batch, seq, d = x.shape
    t = batch * seq
    depth = w_in.shape[0]
    pool_width = len(POOL_WINDOWS) * POOL_GC
    wqb, wkv, wm = SWA_HQ * HEAD_DIM, SWA_HKV * HEAD_DIM, MOBA_HEADS * HEAD_DIM

    names = ["xa", "qb", "kb", "vb", "qm", "km", "vm", "ga", "gb", "gc"]
    widths = [pool_width, wqb, wkv, wkv, wm, wm, wm, d, d, d]
    src, o = {}, 0
    for nme, w in zip(names, widths):
        src[nme] = (o, w)
        o += w
    order = ["ga", "gb", "gc", "qb", "qm", "km", "vm", "xa", "kb", "vb"]
    cols, o = {}, 0
    for nme in order:
        cols[nme] = o
        o += src[nme][1]
    tile_map = [0] * (o // IN_PROJ_TN)
    for nme in order:
        if nme == "vb":
            continue
        width = src[nme][1] + (src["vb"][1] if nme == "kb" else 0)
        for k in range(width // IN_PROJ_TN):
            tile_map[src[nme][0] // IN_PROJ_TN + k] = cols[nme] // IN_PROJ_TN + k
    tile_map = jnp.asarray(tile_map, jnp.int32)

    cos_t, sin_t = _rope_tables(seq)
    x2d = x.reshape(t, d)
    h = rmsnorm_cast(x2d, norm_mix[0])
    xs = None

    for l in range(depth):
        proj = in_proj(h, w_in, l, tile_map)

        def tile2(g):
            return jnp.concatenate([g, g], axis=0)

        gains = jnp.stack([tile2(swa_q_norm[l]), tile2(swa_k_norm[l]), tile2(moba_q_norm[l]), tile2(moba_k_norm[l])]
                          + [jnp.zeros((LANES,), F32)] * 4)
        qb, kb, qm, km, kmean = qk_prep(proj, cols, cos_t, sin_t, gains, seq)
        y_a = pool_mixer(proj, cols["xa"], pool_w[l], pool_scale[l], seq)
        y_b = swa_attention(qb, kb, proj, cols["vb"], swa_sinks[l], seq)
        gate_mats = moba_gate_mats(kmean, batch, seq // MOBA_BLOCK)
        y_c = moba_attention(qm, km, proj, cols["vm"], gate_mats, batch, seq)
        x2d, h_ffn = out_proj(x2d, y_a, y_b, y_c, proj, cols["ga"],
                              w_up_a[l].astype(BF16), w_up_b[l].astype(BF16), w_up_c[l].astype(BF16),
                              w_o[l].astype(BF16), norm_ffn[l])

        wr_t = jnp.concatenate([w_router_expert[l].T, w_router_group[l].T,
                                jnp.zeros((LANES - N_EXPERTS - N_GROUPS, d), F32)], axis=0).astype(BF16)
        route, counts = router(h_ffn, wr_t)
        dest, wts, blk_exp, nvalid, next_exp, n_slots = _moe_plan(route, counts, t)
        xs = dispatch(h_ffn, dest, jnp.zeros((n_slots, d), F32) if xs is None else xs)
        ys = experts(xs, blk_exp, nvalid, next_exp, w_exp_gate, w_exp_up, w_exp_down, l)
        last = l + 1 == depth
        outs = combine(x2d, ys, dest, wts, norm_mix[0 if last else l + 1], emit_norm=not last)
        x2d = outs[0]
        h = None if last else outs[1]
    return x2d.reshape(batch, seq, d)
```

```python
import functools

import jax
import jax.numpy as jnp
from jax import lax
from jax.experimental import pallas as pl
from jax.experimental.pallas import tpu as pltpu

F32 = jnp.float32
BF16 = jnp.bfloat16

HEAD_DIM = 64
ROPE_THETA = 10000.0
RMS_EPS = 1e-6
NEG_INF = -1e30

POOL_WINDOWS = (2, 4, 8, 16)
POOL_GC = 128

SWA_HQ = 16
SWA_HKV = 2
SWA_BLOCK = 128

MOBA_HEADS = 8
MOBA_BLOCK = 256
MOBA_TOPK = 3

N_GROUPS = 4
EXPERTS_PER_GROUP = 8
N_EXPERTS = 32
MOE_BLOCK = 256

LANES = 128
VMEM_LIMIT = 56 * 1024 * 1024

_NT = (((1,), (1,)), ((), ()))


def _params(n_axes, vmem=VMEM_LIMIT):
    return pltpu.CompilerParams(dimension_semantics=("arbitrary",) * n_axes, vmem_limit_bytes=vmem)


def _rmsnorm_kernel(x_ref, g_ref, o_ref):
    x = x_ref[...]
    ms = jnp.mean(x * x, axis=-1, keepdims=True)
    o_ref[...] = (x * lax.rsqrt(ms + RMS_EPS) * g_ref[...]).astype(o_ref.dtype)


def rmsnorm_cast(x2d, gain, tm=512):
    t, d = x2d.shape
    return pl.pallas_call(
        _rmsnorm_kernel,
        grid=(t // tm,),
        in_specs=[pl.BlockSpec((tm, d), lambda i: (i, 0)), pl.BlockSpec((1, d), lambda i: (0, 0))],
        out_specs=pl.BlockSpec((tm, d), lambda i: (i, 0)),
        out_shape=jax.ShapeDtypeStruct((t, d), BF16),
        compiler_params=_params(1),
        name="rmsnorm",
    )(x2d, gain.reshape(1, d))


IN_PROJ_TN = 256


def _in_proj_kernel(tile_map_ref, a_ref, w_ref, o_ref):
    del tile_map_ref
    o_ref[...] = jnp.dot(a_ref[...], w_ref[...].astype(BF16), preferred_element_type=F32).astype(o_ref.dtype)


def in_proj(h, w_in, layer, tile_map, tm=2048):
    t, k = h.shape
    n = w_in.shape[2]
    tn = IN_PROJ_TN
    tm = min(tm, t)
    grid_spec = pltpu.PrefetchScalarGridSpec(
        num_scalar_prefetch=1,
        grid=(t // tm, n // tn),
        in_specs=[pl.BlockSpec((tm, k), lambda i, j, tmap: (i, 0)),
                  pl.BlockSpec((None, k, tn), lambda i, j, tmap: (layer, 0, j))],
        out_specs=pl.BlockSpec((tm, tn), lambda i, j, tmap: (i, tmap[j])),
    )
    return pl.pallas_call(
        _in_proj_kernel,
        grid_spec=grid_spec,
        out_shape=jax.ShapeDtypeStruct((t, n), BF16),
        compiler_params=_params(2),
        name="in_proj",
    )(tile_map, h, w_in)


def _norm_rope(x, gain, cos, sin_signed, blockdiag, first_half):
    x2 = x * x
    hi = x2.astype(BF16)
    lo = (x2 - hi.astype(F32)).astype(BF16)
    ss = jnp.dot(hi, blockdiag, preferred_element_type=F32) + jnp.dot(lo, blockdiag, preferred_element_type=F32)
    y = x * lax.rsqrt(ss * (1.0 / HEAD_DIM) + RMS_EPS) * gain
    rot = jnp.where(first_half, pltpu.roll(y, LANES - HEAD_DIM // 2, 1), pltpu.roll(y, HEAD_DIM // 2, 1))
    return y * cos + rot * sin_signed


def _prep_kernel(qb_ref, kb_ref, qm_ref, km_ref, cos_ref, sin_ref, gains_ref,
                 qb_o, kb_o, qm_o, km_o, kmean_o, *, scale):
    tm = qb_ref.shape[0]
    cos = cos_ref[...]
    sin = sin_ref[...]
    lane = lax.broadcasted_iota(jnp.int32, (tm, LANES), 1)
    first_half = (lane % HEAD_DIM) < (HEAD_DIM // 2)
    r = lax.broadcasted_iota(jnp.int32, (LANES, LANES), 0) // HEAD_DIM
    c = lax.broadcasted_iota(jnp.int32, (LANES, LANES), 1) // HEAD_DIM
    blockdiag = (r == c).astype(BF16)

    def run(src, dst, gain_row, mult, n_groups):
        outs = []
        gain = gains_ref[gain_row:gain_row + 1, :]
        for g in range(n_groups):
            sl = slice(g * LANES, (g + 1) * LANES)
            y = _norm_rope(src[:, sl].astype(F32), gain, cos, sin, blockdiag, first_half)
            if mult != 1.0:
                y = y * mult
            dst[:, sl] = y.astype(dst.dtype)
            outs.append(y)
        return outs

    run(qb_ref, qb_o, 0, scale, qb_ref.shape[1] // LANES)
    run(kb_ref, kb_o, 1, 1.0, kb_ref.shape[1] // LANES)
    run(qm_ref, qm_o, 2, scale, qm_ref.shape[1] // LANES)
    kms = run(km_ref, km_o, 3, 1.0, km_ref.shape[1] // LANES)
    for g, y in enumerate(kms):
        for blk in range(tm // MOBA_BLOCK):
            s = jnp.sum(y[blk * MOBA_BLOCK:(blk + 1) * MOBA_BLOCK, :], axis=0, keepdims=True)
            kmean_o[blk, :, g * LANES:(g + 1) * LANES] = s * (1.0 / MOBA_BLOCK)


def qk_prep(proj, cols, cos_t, sin_t, gains, seq, tm=512):
    t = proj.shape[0]
    tiles_per_seq = seq // tm
    wqb, wkb, wqm, wkm = SWA_HQ * HEAD_DIM, SWA_HKV * HEAD_DIM, MOBA_HEADS * HEAD_DIM, MOBA_HEADS * HEAD_DIM

    def col_spec(width, off):
        return pl.BlockSpec((tm, width), lambda i, o=off // width: (i, o))

    tab_spec = pl.BlockSpec((tm, LANES), lambda i: (i % tiles_per_seq, 0))
    return pl.pallas_call(
        functools.partial(_prep_kernel, scale=HEAD_DIM ** -0.5),
        grid=(t // tm,),
        in_specs=[col_spec(wqb, cols["qb"]), col_spec(wkb, cols["kb"]), col_spec(wqm, cols["qm"]),
                  col_spec(wkm, cols["km"]), tab_spec, tab_spec, pl.BlockSpec((8, LANES), lambda i: (0, 0))],
        out_specs=[pl.BlockSpec((tm, wqb), lambda i: (i, 0)), pl.BlockSpec((tm, wkb), lambda i: (i, 0)),
                   pl.BlockSpec((tm, wqm), lambda i: (i, 0)), pl.BlockSpec((tm, wkm), lambda i: (i, 0)),
                   pl.BlockSpec((tm // MOBA_BLOCK, 1, wkm), lambda i: (i, 0, 0))],
        out_shape=[jax.ShapeDtypeStruct((t, wqb), BF16), jax.ShapeDtypeStruct((t, wkb), BF16),
                   jax.ShapeDtypeStruct((t, wqm), BF16), jax.ShapeDtypeStruct((t, wkm), BF16),
                   jax.ShapeDtypeStruct((t // MOBA_BLOCK, 1, wkm), F32)],
        compiler_params=_params(1),
        name="qk_prep",
    )(proj, proj, proj, proj, cos_t, sin_t, gains)


def _pool_kernel(cur_ref, prev_ref, pw_ref, ps_ref, o_ref, *, tiles_per_seq):
    tm = cur_ref.shape[0]
    i = pl.program_id(0)
    tile_in_seq = i % tiles_per_seq
    row = lax.broadcasted_iota(jnp.int32, (tm, tm), 0)
    col = lax.broadcasted_iota(jnp.int32, (tm, tm), 1)
    pos = tile_in_seq * tm + lax.broadcasted_iota(jnp.int32, (tm, 1), 0)
    has_prev = (tile_in_seq > 0).astype(F32)
    for g, w in enumerate(POOL_WINDOWS):
        sl = slice(g * POOL_GC, (g + 1) * POOL_GC)
        cur = cur_ref[:, sl]
        prev = prev_ref[:, sl]
        band_cur = ((row >= col) & (row - col < w)).astype(BF16)
        band_prev = (row + tm - col < w).astype(BF16)
        win = jnp.dot(band_cur, cur, preferred_element_type=F32)
        win = win + has_prev * jnp.dot(band_prev, prev, preferred_element_type=F32)
        cnt = jnp.minimum(pos + 1, w).astype(F32)
        pooled = (win / cnt - cur.astype(F32)).astype(BF16)
        y = jnp.dot(pooled, pw_ref[g], preferred_element_type=F32) * ps_ref[:, sl]
        o_ref[:, sl] = y.astype(o_ref.dtype)


def pool_mixer(proj, col_off, pool_w, pool_scale, seq, tm=256):
    t = proj.shape[0]
    width = len(POOL_WINDOWS) * POOL_GC
    cb = col_off // width
    tiles_per_seq = seq // tm
    return pl.pallas_call(
        functools.partial(_pool_kernel, tiles_per_seq=tiles_per_seq),
        grid=(t // tm,),
        in_specs=[pl.BlockSpec((tm, width), lambda i: (i, cb)),
                  pl.BlockSpec((tm, width), lambda i: (jnp.maximum(i - 1, 0), cb)),
                  pl.BlockSpec((len(POOL_WINDOWS), POOL_GC, POOL_GC), lambda i: (0, 0, 0)),
                  pl.BlockSpec((1, width), lambda i: (0, 0))],
        out_specs=pl.BlockSpec((tm, width), lambda i: (i, 0)),
        out_shape=jax.ShapeDtypeStruct((t, width), BF16),
        compiler_params=_params(1),
        name="pool_mixer",
    )(proj, proj, pool_w.astype(BF16), pool_scale.reshape(1, width))


def _swa_kernel(q_ref, kc_ref, kp_ref, vc_ref, vp_ref, sink_ref, o_ref, *, blocks_per_seq):
    blk = SWA_BLOCK
    nk = 2 * blk
    rep = SWA_HQ // SWA_HKV
    ppg = rep // 2
    i = pl.program_id(0)
    not_first = (i % blocks_per_seq) > 0
    lane = lax.broadcasted_iota(jnp.int32, (nk, LANES), 1)

    kk = jnp.concatenate([kp_ref[...], kc_ref[...]], axis=0).astype(F32)
    k0_lo = jnp.where(lane < HEAD_DIM, kk, 0.0)
    k1_hi = jnp.where(lane >= HEAD_DIM, kk, 0.0)
    kmats = ((k0_lo.astype(BF16), pltpu.roll(k0_lo, HEAD_DIM, 1).astype(BF16)),
             (pltpu.roll(k1_hi, HEAD_DIM, 1).astype(BF16), k1_hi.astype(BF16)))
    vt = jnp.concatenate([vp_ref[...], vc_ref[...]], axis=0).astype(F32).T
    ones_rows = (lax.broadcasted_iota(jnp.int32, (VT_ROWS - HEAD_DIM, nk), 0) == 0).astype(F32)
    vts = [jnp.concatenate([vt[g * HEAD_DIM:(g + 1) * HEAD_DIM], ones_rows], axis=0).astype(BF16)
           for g in range(SWA_HKV)]
    qg = [jnp.concatenate([q_ref[:, (g * ppg + pp) * LANES:(g * ppg + pp + 1) * LANES] for pp in range(ppg)], axis=0)
          for g in range(SWA_HKV)]

    ncol = ppg * blk
    key = lax.broadcasted_iota(jnp.int32, (nk, ncol), 0)
    col = lax.broadcasted_iota(jnp.int32, (nk, ncol), 1)
    dist = col % blk + blk - key
    band = (dist >= 0) & (dist < blk) & (not_first | (key >= blk))
    seg = lax.broadcasted_iota(jnp.int32, (1, ncol), 1) // blk

    tasks = [(g, par) for g in range(SWA_HKV) for par in range(2)]

    def scores(g, par):
        return lax.dot_general(kmats[g][par], qg[g], _NT, preferred_element_type=F32)

    outs = {}
    st_next = scores(*tasks[0])
    for n, (g, par) in enumerate(tasks):
        st = st_next
        if n + 1 < len(tasks):
            st_next = scores(*tasks[n + 1])
        sink = jnp.zeros((1, ncol), F32)
        for pp in range(ppg):
            sink = jnp.where(seg == pp, sink_ref[0, g * rep + 2 * pp + par], sink)
        sb = jnp.where(band, st, NEG_INF).astype(BF16)
        mb = jnp.maximum(jnp.max(sb, axis=0, keepdims=True), sink.astype(BF16))
        pm = jnp.exp(sb - mb)
        pv = jnp.dot(vts[g], pm, preferred_element_type=F32)
        den = pv[HEAD_DIM:HEAD_DIM + 1] + jnp.exp(sink - mb.astype(F32))
        outs[(g, par)] = pv[0:HEAD_DIM] / den
    for p in range(SWA_HQ // 2):
        g, pp = p // ppg, p % ppg
        pair_t = jnp.concatenate([outs[(g, 0)][:, pp * blk:(pp + 1) * blk], outs[(g, 1)][:, pp * blk:(pp + 1) * blk]],
                                 axis=0)
        o_ref[:, p * LANES:(p + 1) * LANES] = pair_t.T.astype(o_ref.dtype)


def swa_attention(q, k, proj, v_col, sinks, seq):
    t = q.shape[0]
    blk = SWA_BLOCK
    bps = seq // blk
    kvw = SWA_HKV * HEAD_DIM
    vb = v_col // kvw

    def prev(i):
        return jnp.maximum(i - 1, 0)

    return pl.pallas_call(
        functools.partial(_swa_kernel, blocks_per_seq=bps),
        grid=(t // blk,),
        in_specs=[pl.BlockSpec((blk, q.shape[1]), lambda i: (i, 0)),
                  pl.BlockSpec((blk, kvw), lambda i: (i, 0)),
                  pl.BlockSpec((blk, kvw), lambda i: (prev(i), 0)),
                  pl.BlockSpec((blk, kvw), lambda i: (i, vb)),
                  pl.BlockSpec((blk, kvw), lambda i: (prev(i), vb)),
                  pl.BlockSpec((1, SWA_HQ), lambda i: (0, 0), memory_space=pltpu.SMEM)],
        out_specs=pl.BlockSpec((blk, q.shape[1]), lambda i: (i, 0)),
        out_shape=jax.ShapeDtypeStruct(q.shape, BF16),
        compiler_params=_params(1),
        name="swa_attention",
    )(q, k, k, proj, proj, sinks.reshape(1, SWA_HQ).astype(F32))


VT_ROWS = HEAD_DIM + 16
MOBA_TASK_ROWS = 512
MOBA_TASK_COLS = 512
MOBA_ITER_KEYS = 2048


def _moba_kernel(q_ref, k_ref, v_ref, km_ref, o_ref, vt_ref, ke_ref, ko_ref, m_ref, acc_ref,
                 *, nblk, qt):
    blk = MOBA_BLOCK
    sub = qt // blk
    npair = q_ref.shape[1] // LANES
    qi = pl.program_id(2)
    lane = lax.broadcasted_iota(jnp.int32, (blk, LANES), 1)
    low = lane < HEAD_DIM

    @pl.when(qi == 0)
    def _():
        ones_row = (lax.broadcasted_iota(jnp.int32, (VT_ROWS - HEAD_DIM, vt_ref.shape[2]), 0) == 0).astype(BF16)
        for h in range(2 * npair):
            vt_ref[h, HEAD_DIM:VT_ROWS, :] = ones_row

        def fill(j, c):
            off = pl.multiple_of(j * blk, blk)
            vt = v_ref[pl.ds(off, blk), :].astype(F32).T.astype(BF16)
            for h in range(2 * npair):
                vt_ref[h, 0:HEAD_DIM, pl.ds(off, blk)] = vt[h * HEAD_DIM:(h + 1) * HEAD_DIM, :]
            for p in range(npair):
                kf = k_ref[pl.ds(off, blk), p * LANES:(p + 1) * LANES].astype(F32)
                ke_ref[p, pl.ds(off, blk), :] = jnp.where(low, kf, (lane - HEAD_DIM == j).astype(F32)).astype(BF16)
                ko_ref[p, pl.ds(off, blk), :] = jnp.where(low, (lane == j).astype(F32), kf).astype(BF16)
            return c
        lax.fori_loop(0, nblk, fill, 0)

    ncol = npair * qt
    block_of_row = lax.broadcasted_iota(jnp.int32, (HEAD_DIM, ncol), 0)
    own = qi * sub + (lax.broadcasted_iota(jnp.int32, (1, ncol), 1) % qt) // blk

    def block_bias(gates_t):
        g = jnp.where(block_of_row < own, gates_t, -jnp.inf)
        sel = block_of_row == own
        for _ in range(MOBA_TOPK):
            mx = jnp.max(g, axis=0, keepdims=True)
            idx = jnp.min(jnp.where(g == mx, block_of_row, 1 << 20), axis=0, keepdims=True)
            hit = (block_of_row == idx) & (mx > -jnp.inf)
            sel = sel | hit
            g = jnp.where(hit, -jnp.inf, g)
        return jnp.where(sel, 0.0, NEG_INF)

    gates_t = jnp.concatenate(
        [lax.dot_general(km_ref[p], q_ref[:, p * LANES:(p + 1) * LANES], _NT, preferred_element_type=F32)
         for p in range(npair)], axis=1)
    bias_t = jnp.concatenate([block_bias(gates_t[0:HEAD_DIM]), block_bias(gates_t[HEAD_DIM:])], axis=0)
    lane_q = lax.broadcasted_iota(jnp.int32, (qt, LANES), 1)
    low_q = lane_q < HEAD_DIM
    q_ext = []
    for p in range(npair):
        bias = bias_t[:, p * qt:(p + 1) * qt].T
        qf = q_ref[:, p * LANES:(p + 1) * LANES].astype(F32)
        q_ext.append((jnp.where(low_q, qf, bias).astype(BF16),
                      jnp.where(low_q, bias, qf).astype(BF16)))

    m_ref[...] = jnp.full(m_ref.shape, -jnp.inf, F32)
    acc_ref[...] = jnp.zeros(acc_ref.shape, F32)
    rows = sub * blk

    def steps(first_group, n_groups, trows, causal):
        start = first_group * rows
        tcols = MOBA_TASK_COLS
        tasks = [(pl.multiple_of(start + part * trows, trows), part, h, ch * tcols)
                 for part in range(n_groups * rows // trows) for h in range(2 * npair) for ch in range(qt // tcols)]

        def scores(off, part, h, c0):
            k_ext = (ke_ref if h % 2 == 0 else ko_ref)[h // 2, pl.ds(off, trows), :]
            return lax.dot_general(k_ext, q_ext[h // 2][h % 2][c0:c0 + tcols], _NT,
                                   preferred_element_type=F32)

        ahead = 1
        pending = [scores(*tasks[n]) for n in range(min(ahead, len(tasks)))]
        for n, (off, part, h, c0) in enumerate(tasks):
            st = pending.pop(0)
            if n + ahead < len(tasks):
                pending.append(scores(*tasks[n + ahead]))
            if causal:
                key_idx = lax.broadcasted_iota(jnp.int32, (trows, tcols), 0) + part * trows
                st = jnp.where(key_idx <= lax.broadcasted_iota(jnp.int32, (trows, tcols), 1) + c0, st, NEG_INF)
            sb = st.astype(BF16)
            m = m_ref[h, :, c0:c0 + tcols]
            m_new = jnp.maximum(m, jnp.max(sb, axis=0, keepdims=True).astype(F32))
            alpha = jnp.exp(m - m_new)
            pm = jnp.exp(sb - m_new.astype(BF16))
            pv = jnp.dot(vt_ref[h, :, pl.ds(off, trows)], pm, preferred_element_type=F32)
            acc_ref[h, :, c0:c0 + tcols] = acc_ref[h, :, c0:c0 + tcols] * alpha + pv
            m_ref[h, :, c0:c0 + tcols] = m_new

    gpi = MOBA_ITER_KEYS // rows

    def body(u, c):
        steps(gpi * u, gpi, MOBA_TASK_ROWS, False)
        return c

    lax.fori_loop(0, qi // gpi, body, 0)

    def tail(jt, c):
        steps(jt, 1, MOBA_TASK_ROWS, False)
        return c

    lax.fori_loop((qi // gpi) * gpi, qi, tail, 0)

    steps(qi, 1, MOBA_TASK_ROWS, True)
    outs = [acc_ref[h, 0:HEAD_DIM, :] / acc_ref[h, HEAD_DIM:HEAD_DIM + 1, :] for h in range(2 * npair)]
    o_ref[...] = jnp.concatenate(outs, axis=0).T.astype(o_ref.dtype)


def moba_attention(q, k, proj, v_col, gate_mats, batch, seq, qt=512, npair=2):
    t, width = q.shape
    nblk = seq // MOBA_BLOCK
    qt = min(qt, seq)
    gw = npair * LANES
    q3 = q.reshape(batch, seq, width)
    k3 = k.reshape(batch, seq, width)
    p3 = proj.reshape(batch, seq, proj.shape[1])
    vb = v_col // gw

    def resident(shape, index_map):
        return pl.BlockSpec(shape, index_map, pipeline_mode=pl.Buffered(1))

    out = pl.pallas_call(
        functools.partial(_moba_kernel, nblk=nblk, qt=qt),
        grid=(batch, width // gw, seq // qt),
        in_specs=[pl.BlockSpec((None, qt, gw), lambda b, g, i: (b, i, g)),
                  resident((None, seq, gw), lambda b, g, i: (b, 0, g)),
                  resident((None, seq, gw), lambda b, g, i: (b, 0, vb + g)),
                  pl.BlockSpec((None, npair, LANES, LANES), lambda b, g, i: (b, g, 0, 0))],
        out_specs=pl.BlockSpec((None, qt, gw), lambda b, g, i: (b, i, g)),
        out_shape=jax.ShapeDtypeStruct((batch, seq, width), BF16),
        scratch_shapes=[pltpu.VMEM((2 * npair, VT_ROWS, seq), BF16),
                        pltpu.VMEM((npair, seq, LANES), BF16), pltpu.VMEM((npair, seq, LANES), BF16),
                        pltpu.VMEM((2 * npair, 1, qt), F32),
                        pltpu.VMEM((2 * npair, VT_ROWS, qt), F32)],
        compiler_params=_params(3),
        name="moba_attention",
    )(q3, k3, p3, gate_mats)
    return out.reshape(t, width)


def moba_gate_mats(kmean, batch, nblk):
    km = kmean.reshape(batch, nblk, MOBA_HEADS // 2, 2, HEAD_DIM).astype(BF16)
    km = jnp.transpose(km, (0, 2, 3, 1, 4))
    pad_rows = LANES // 2 - nblk
    even = jnp.pad(km[:, :, 0], ((0, 0), (0, 0), (LANES // 2, pad_rows), (0, HEAD_DIM)))
    odd = jnp.pad(km[:, :, 1], ((0, 0), (0, 0), (0, LANES // 2 + pad_rows), (HEAD_DIM, 0)))
    return even + odd


def _outproj_kernel(x_ref, ya_ref, yb_ref, yc_ref, ga_ref, gb_ref, gc_ref,
                    wa_ref, wb_ref, wc_ref, wo_ref, gain_ref, xo_ref, h_ref):
    def gate(g_ref):
        return jax.nn.sigmoid(g_ref[...].astype(F32))

    merged = gate(ga_ref) * jnp.dot(ya_ref[...], wa_ref[...], preferred_element_type=F32)
    merged = merged + gate(gb_ref) * jnp.dot(yb_ref[...], wb_ref[...], preferred_element_type=F32)
    merged = merged + gate(gc_ref) * jnp.dot(yc_ref[...], wc_ref[...], preferred_element_type=F32)
    xn = x_ref[...] + jnp.dot(merged.astype(BF16), wo_ref[...], preferred_element_type=F32)
    xo_ref[...] = xn
    ms = jnp.mean(xn * xn, axis=-1, keepdims=True)
    h_ref[...] = xn * lax.rsqrt(ms + RMS_EPS) * gain_ref[...]


def out_proj(x2d, ya, yb, yc, proj, gate_col, wa, wb, wc, wo, gain, tm=256):
    t, d = x2d.shape
    gb0 = gate_col // d

    def rows(width):
        return pl.BlockSpec((tm, width), lambda i: (i, 0))

    def whole(w):
        return pl.BlockSpec(w.shape, lambda i: (0, 0), pipeline_mode=pl.Buffered(1))

    return pl.pallas_call(
        _outproj_kernel,
        grid=(t // tm,),
        in_specs=[rows(d), rows(ya.shape[1]), rows(yb.shape[1]), rows(yc.shape[1]),
                  pl.BlockSpec((tm, d), lambda i: (i, gb0)),
                  pl.BlockSpec((tm, d), lambda i: (i, gb0 + 1)),
                  pl.BlockSpec((tm, d), lambda i: (i, gb0 + 2)),
                  whole(wa), whole(wb), whole(wc), whole(wo),
                  pl.BlockSpec((1, d), lambda i: (0, 0))],
        out_specs=[rows(d), rows(d)],
        out_shape=[jax.ShapeDtypeStruct((t, d), F32), jax.ShapeDtypeStruct((t, d), F32)],
        compiler_params=_params(1),
        name="out_proj",
    )(x2d, ya, yb, yc, proj, proj, proj, wa, wb, wc, wo, gain.reshape(1, d))


def _router_kernel(h_ref, wr_ref, route_ref, count_ref, run_ref):
    tm = h_ref.shape[0]
    epg = EXPERTS_PER_GROUP

    @pl.when(pl.program_id(0) == 0)
    def _():
        run_ref[...] = jnp.zeros(run_ref.shape, F32)

    logits = lax.dot_general(wr_ref[...], h_ref[...].astype(BF16), _NT, preferred_element_type=F32)
    row8 = lax.broadcasted_iota(jnp.int32, (epg, tm), 0)
    glog = jnp.where(row8 < N_GROUPS, logits[N_EXPERTS:N_EXPERTS + epg, :], -jnp.inf)
    gmax = jnp.max(glog, axis=0, keepdims=True)
    gsel = jnp.min(jnp.where(glog == gmax, row8, epg), axis=0, keepdims=True)
    p_grp = 1.0 / jnp.sum(jnp.exp(glog - gmax), axis=0, keepdims=True)
    within = jnp.zeros((epg, tm), F32)
    for g in range(N_GROUPS):
        within = jnp.where(gsel == g, logits[g * epg:(g + 1) * epg, :], within)
    wmax = jnp.max(within, axis=0, keepdims=True)
    e = jnp.exp(within - wmax)
    prob = e / jnp.sum(e, axis=0, keepdims=True)
    p1 = jnp.max(prob, axis=0, keepdims=True)
    i1 = jnp.min(jnp.where(prob == p1, row8, epg), axis=0, keepdims=True)
    rest = jnp.where(row8 == i1, -jnp.inf, prob)
    p2 = jnp.max(rest, axis=0, keepdims=True)
    i2 = jnp.min(jnp.where(rest == p2, row8, epg), axis=0, keepdims=True)
    norm = p1 + p2
    w1 = p_grp * p1 / norm
    w2 = p_grp * p2 / norm
    e1 = gsel * epg + i1
    e2 = gsel * epg + i2

    row32 = lax.broadcasted_iota(jnp.int32, (N_EXPERTS, tm), 0)
    oh1 = (row32 == e1).astype(F32)
    oh2 = (row32 == e2).astype(F32)
    oh = oh1 + oh2
    before = (lax.broadcasted_iota(jnp.int32, (tm, tm), 0) < lax.broadcasted_iota(jnp.int32, (tm, tm), 1)).astype(BF16)
    prior = jnp.dot(oh.astype(BF16), before, preferred_element_type=F32) + run_ref[:, 0:1]
    r1 = jnp.sum(oh1 * prior, axis=0, keepdims=True)
    r2 = jnp.sum(oh2 * prior, axis=0, keepdims=True)
    run_new = run_ref[...] + jnp.sum(oh, axis=1, keepdims=True)
    run_ref[...] = run_new
    count_ref[...] = run_new
    zero = jnp.zeros((1, tm), F32)
    route_ref[...] = jnp.concatenate([e1.astype(F32), e2.astype(F32), r1, r2, w1, w2, zero, zero], axis=0)


def router(h, wr_t, tm=512):
    t, d = h.shape
    return pl.pallas_call(
        _router_kernel,
        grid=(t // tm,),
        in_specs=[pl.BlockSpec((tm, d), lambda i: (i, 0)), pl.BlockSpec(wr_t.shape, lambda i: (0, 0))],
        out_specs=[pl.BlockSpec((8, tm), lambda i: (0, i)), pl.BlockSpec((N_EXPERTS, LANES), lambda i: (0, 0))],
        out_shape=[jax.ShapeDtypeStruct((8, t), F32), jax.ShapeDtypeStruct((N_EXPERTS, LANES), F32)],
        scratch_shapes=[pltpu.VMEM((N_EXPERTS, LANES), F32)],
        compiler_params=_params(1),
        name="router",
    )(h, wr_t)


def _row_copy(src_ref, src_row, dst_ref, dst_row, sem):
    return pltpu.make_async_copy(src_ref.at[pl.ds(src_row, 1)], dst_ref.at[pl.ds(dst_row, 1)], sem)


def _dispatch_kernel(dest_ref, h_ref, init_ref, xs_ref, sem):
    del init_ref
    tm = h_ref.shape[0]

    def start(r, c):
        for k in range(2):
            _row_copy(h_ref, r, xs_ref, dest_ref[k, r], sem).start(priority=k)
        return c

    def wait(r, c):
        for k in range(2):
            _row_copy(h_ref, r, xs_ref, dest_ref[k, r], sem).wait()
        return c

    for r in range(tm):
        start(r, 0)
    lax.fori_loop(0, tm, wait, 0, unroll=8)


def dispatch(h, dest, slots_init, tm=256):
    t, d = h.shape
    n_slots = slots_init.shape[0]
    dest3 = dest.reshape(2, t // tm, tm).transpose(1, 0, 2)
    return pl.pallas_call(
        _dispatch_kernel,
        grid=(t // tm,),
        in_specs=[pl.BlockSpec((None, 2, tm), lambda i: (i, 0, 0), memory_space=pltpu.SMEM),
                  pl.BlockSpec((tm, d), lambda i: (i, 0)),
                  pl.BlockSpec(memory_space=pl.ANY)],
        out_specs=pl.BlockSpec(memory_space=pl.ANY),
        out_shape=jax.ShapeDtypeStruct((n_slots, d), h.dtype),
        input_output_aliases={2: 0},
        scratch_shapes=[pltpu.SemaphoreType.DMA],
        compiler_params=_params(1),
        name="moe_dispatch",
    )(dest3, h, slots_init)


CAST_ROWS = 256


def _cast_rows(src_ref, dst_ref):
    def body(c, carry):
        sl = pl.ds(pl.multiple_of(c * CAST_ROWS, CAST_ROWS), CAST_ROWS)
        dst_ref[sl, :] = src_ref[sl, :].astype(dst_ref.dtype)
        return carry
    lax.fori_loop(0, src_ref.shape[0] // CAST_ROWS, body, 0)


def _expert_kernel(bexp_ref, nvalid_ref, next_ref, xs_ref, wg_hbm, wu_hbm, wd_hbm, ys_ref,
                   wg_st, wu_st, wd_st, wg_bf, wu_bf, wd_bf, state_ref, sems, *, layer):
    n = pl.program_id(0)
    nv = nvalid_ref[n]
    e = bexp_ref[n]

    def fetch(expert):
        return [pltpu.make_async_copy(src.at[layer, expert], dst, sems.at[k])
                for k, (src, dst) in enumerate(((wg_hbm, wg_st), (wu_hbm, wu_st), (wd_hbm, wd_st)))]

    @pl.when(n == 0)
    def _():
        state_ref[0] = -1
        state_ref[1] = -1

    @pl.when((nv > 0) & (state_ref[0] != e))
    def _():
        @pl.when(state_ref[1] != e)
        def _():
            for c in fetch(e):
                c.start()
            state_ref[1] = e

        for c in fetch(e):
            c.wait()
        _cast_rows(wg_st, wg_bf)
        _cast_rows(wu_st, wu_bf)
        _cast_rows(wd_st, wd_bf)
        state_ref[0] = e
        nxt = next_ref[n]

        @pl.when(nxt >= 0)
        def _():
            for c in fetch(nxt):
                c.start()
            state_ref[1] = nxt

    @pl.when(nv > 0)
    def _():
        row = lax.broadcasted_iota(jnp.int32, xs_ref.shape, 0)
        x = jnp.where(row < nv, xs_ref[...], 0.0).astype(BF16)
        g = jnp.dot(x, wg_bf[...], preferred_element_type=F32)
        u = jnp.dot(x, wu_bf[...], preferred_element_type=F32)
        act = (jax.nn.silu(g) * u).astype(BF16)
        ys_ref[...] = jnp.dot(act, wd_bf[...], preferred_element_type=F32)

    @pl.when(nv <= 0)
    def _():
        ys_ref[...] = jnp.zeros(ys_ref.shape, ys_ref.dtype)


def experts(xs, blk_exp, nvalid, next_exp, wg, wu, wd, layer):
    n_slots, d = xs.shape
    ff = wg.shape[3]
    nb = n_slots // MOE_BLOCK
    hbm = pl.BlockSpec(memory_space=pl.ANY)
    grid_spec = pltpu.PrefetchScalarGridSpec(
        num_scalar_prefetch=3,
        grid=(nb,),
        in_specs=[pl.BlockSpec((MOE_BLOCK, d), lambda n, be, nv, nx: (n, 0)), hbm, hbm, hbm],
        out_specs=pl.BlockSpec((MOE_BLOCK, d), lambda n, be, nv, nx: (n, 0)),
        scratch_shapes=[pltpu.VMEM((d, ff), F32), pltpu.VMEM((d, ff), F32), pltpu.VMEM((ff, d), F32),
                        pltpu.VMEM((d, ff), BF16), pltpu.VMEM((d, ff), BF16), pltpu.VMEM((ff, d), BF16),
                        pltpu.SMEM((2,), jnp.int32), pltpu.SemaphoreType.DMA((3,))],
    )
    return pl.pallas_call(
        functools.partial(_expert_kernel, layer=layer),
        grid_spec=grid_spec,
        out_shape=jax.ShapeDtypeStruct((n_slots, d), F32),
        compiler_params=_params(1),
        name="moe_experts",
    )(blk_exp, nvalid, next_exp, xs, wg, wu, wd)


def _combine_kernel(dest_ref, dest_next_ref, x_ref, w_ref, gain_ref, ys_ref, xo_ref, *rest):
    h_ref, buf_ref, sems = rest if len(rest) == 3 else (None,) + rest
    tm = x_ref.shape[0]
    i = pl.program_id(0)
    slot = i % 2

    def gather(d_ref, s, wait):
        def body(r, c):
            for k in range(2):
                cp = _row_copy(ys_ref, d_ref[k, r], buf_ref.at[s, k], r, sems.at[s])
                if wait:
                    cp.wait()
                else:
                    cp.start(priority=k)
            return c
        if wait:
            lax.fori_loop(0, tm, body, 0, unroll=8)
        else:
            for r in range(tm):
                body(r, 0)

    @pl.when(i == 0)
    def _():
        gather(dest_ref, 0, wait=False)

    @pl.when(i + 1 < pl.num_programs(0))
    def _():
        gather(dest_next_ref, 1 - slot, wait=False)

    gather(dest_ref, slot, wait=True)
    w = w_ref[...]
    xn = x_ref[...] + buf_ref[slot, 0] * w[:, 0:1] + buf_ref[slot, 1] * w[:, 1:2]
    xo_ref[...] = xn
    if h_ref is not None:
        ms = jnp.mean(xn * xn, axis=-1, keepdims=True)
        h_ref[...] = (xn * lax.rsqrt(ms + RMS_EPS) * gain_ref[...]).astype(h_ref.dtype)


def combine(x2d, ys, dest, wts, gain, emit_norm, tm=256):
    t, d = x2d.shape
    dest3 = dest.reshape(2, t // tm, tm).transpose(1, 0, 2)
    return pl.pallas_call(
        _combine_kernel,
        grid=(t // tm,),
        in_specs=[pl.BlockSpec((None, 2, tm), lambda i: (i, 0, 0), memory_space=pltpu.SMEM),
                  pl.BlockSpec((None, 2, tm), lambda i: (jnp.minimum(i + 1, t // tm - 1), 0, 0),
                               memory_space=pltpu.SMEM),
                  pl.BlockSpec((tm, d), lambda i: (i, 0)),
                  pl.BlockSpec((tm, 2), lambda i: (i, 0)),
                  pl.BlockSpec((1, d), lambda i: (0, 0)),
                  pl.BlockSpec(memory_space=pl.ANY)],
        out_specs=[pl.BlockSpec((tm, d), lambda i: (i, 0))] * (2 if emit_norm else 1),
        out_shape=[jax.ShapeDtypeStruct((t, d), F32)] + ([jax.ShapeDtypeStruct((t, d), BF16)] if emit_norm else []),
        scratch_shapes=[pltpu.VMEM((2, 2, tm, d), F32), pltpu.SemaphoreType.DMA((2,))],
        compiler_params=_params(1),
        name="moe_combine",
    )(dest3, dest3, x2d, wts, gain.reshape(1, d), ys)


def _rope_tables(seq):
    inv_freq = 1.0 / (ROPE_THETA ** (jnp.arange(0, HEAD_DIM, 2, dtype=F32) / HEAD_DIM))
    ang = jnp.arange(seq, dtype=F32)[:, None] * inv_freq[None, :]
    cos, sin = jnp.cos(ang), jnp.sin(ang)
    cos_t = jnp.concatenate([cos, cos, cos, cos], axis=1)
    sin_t = jnp.concatenate([-sin, sin, -sin, sin], axis=1)
    return cos_t, sin_t


def _moe_plan(route, counts, n_tokens):
    eid = route[0:2].astype(jnp.int32)
    rank = route[2:4].astype(jnp.int32)
    wts = route[4:6].T
    cnt = counts[:, 0].astype(jnp.int32)
    padded = ((cnt + MOE_BLOCK - 1) // MOE_BLOCK) * MOE_BLOCK
    p_ends = jnp.cumsum(padded)
    p_starts = p_ends - padded
    hit = eid[..., None] == jnp.arange(N_EXPERTS, dtype=jnp.int32)
    dest = rank + jnp.sum(jnp.where(hit, p_starts, 0), axis=-1)
    n_blocks = -(-(n_tokens * 2) // MOE_BLOCK) + N_EXPERTS
    blk_start = jnp.arange(n_blocks, dtype=jnp.int32) * MOE_BLOCK
    blk_exp = jnp.minimum(jnp.sum(p_ends[None, :] <= blk_start[:, None], axis=1), N_EXPERTS - 1).astype(jnp.int32)
    nvalid = jnp.clip(cnt[blk_exp] - (blk_start - p_starts[blk_exp]), 0, MOE_BLOCK).astype(jnp.int32)
    ids = jnp.arange(N_EXPERTS, dtype=jnp.int32)
    later_used = (ids[None, :] > ids[:, None]) & (cnt[None, :] > 0)
    next_used = jnp.min(jnp.where(later_used, ids[None, :], N_EXPERTS), axis=1)
    next_exp = jnp.where(next_used < N_EXPERTS, next_used, -1)[blk_exp].astype(jnp.int32)
    return dest, wts, blk_exp, nvalid, next_exp, n_blocks * MOE_BLOCK


def kernel(x, norm_mix, w_in, pool_w, pool_scale, swa_q_norm, swa_k_norm, swa_sinks, moba_q_norm, moba_k_norm,
           w_up_a, w_up_b, w_up_c, w_o, norm_ffn, w_router_group, w_router_expert, w_exp_gate, w_exp_up, w_exp_down):
    batch, seq, d = x.shape
    t = batch * seq
    depth = w_in.shape[0]
    pool_width = len(POOL_WINDOWS) * POOL_GC
    wqb, wkv, wm = SWA_HQ * HEAD_DIM, SWA_HKV * HEAD_DIM, MOBA_HEADS * HEAD_DIM

    names = ["xa", "qb", "kb", "vb", "qm", "km", "vm", "ga", "gb", "gc"]
    widths = [pool_width, wqb, wkv, wkv, wm, wm, wm, d, d, d]
    src, o = {}, 0
    for nme, w in zip(names, widths):
        src[nme] = (o, w)
        o += w
    order = ["ga", "gb", "gc", "qb", "qm", "km", "vm", "xa", "kb", "vb"]
    cols, o = {}, 0
    for nme in order:
        cols[nme] = o
        o += src[nme][1]
    tile_map = [0] * (o // IN_PROJ_TN)
    for nme in order:
        if nme == "vb":
            continue
        width = src[nme][1] + (src["vb"][1] if nme == "kb" else 0)
        for k in range(width // IN_PROJ_TN):
            tile_map[src[nme][0] // IN_PROJ_TN + k] = cols[nme] // IN_PROJ_TN + k
    tile_map = jnp.asarray(tile_map, jnp.int32)

    cos_t, sin_t = _rope_tables(seq)
    x2d = x.reshape(t, d)
    h = rmsnorm_cast(x2d, norm_mix[0])
    xs = None

    for l in range(depth):
        proj = in_proj(h, w_in, l, tile_map)

        def tile2(g):
            return jnp.concatenate([g, g], axis=0)

        gains = jnp.stack([tile2(swa_q_norm[l]), tile2(swa_k_norm[l]), tile2(moba_q_norm[l]), tile2(moba_k_norm[l])]
                          + [jnp.zeros((LANES,), F32)] * 4)
        qb, kb, qm, km, kmean = qk_prep(proj, cols, cos_t, sin_t, gains, seq)
        y_a = pool_mixer(proj, cols["xa"], pool_w[l], pool_scale[l], seq)
        y_b = swa_attention(qb, kb, proj, cols["vb"], swa_sinks[l], seq)
        gate_mats = moba_gate_mats(kmean, batch, seq // MOBA_BLOCK)
        y_c = moba_attention(qm, km, proj, cols["vm"], gate_mats, batch, seq)
        x2d, h_ffn = out_proj(x2d, y_a, y_b, y_c, proj, cols["ga"],
                              w_up_a[l].astype(BF16), w_up_b[l].astype(BF16), w_up_c[l].astype(BF16),
                              w_o[l].astype(BF16), norm_ffn[l])

        wr_t = jnp.concatenate([w_router_expert[l].T, w_router_group[l].T,
                                jnp.zeros((LANES - N_EXPERTS - N_GROUPS, d), F32)], axis=0).astype(BF16)
        route, counts = router(h_ffn, wr_t)
        dest, wts, blk_exp, nvalid, next_exp, n_slots = _moe_plan(route, counts, t)
        xs = dispatch(h_ffn, dest, jnp.zeros((n_slots, d), F32) if xs is None else xs)
        ys = experts(xs, blk_exp, nvalid, next_exp, w_exp_gate, w_exp_up, w_exp_down, l)
        last = l + 1 == depth
        outs = combine(x2d, ys, dest, wts, norm_mix[0 if last else l + 1], emit_norm=not last)
        x2d = outs[0]
        h = None if last else outs[1]
    return x2d.reshape(batch, seq, d)
```

```python
import functools

import jax
import jax.numpy as jnp
from jax import lax
from jax.experimental import pallas as pl
from jax.experimental.pallas import tpu as pltpu

F32 = jnp.float32
BF16 = jnp.bfloat16

HEAD_DIM = 64
ROPE_THETA = 10000.0
RMS_EPS = 1e-6
NEG_INF = -1e30

POOL_WINDOWS = (2, 4, 8, 16)
POOL_GC = 128

SWA_HQ = 16
SWA_HKV = 2
SWA_BLOCK = 128

MOBA_HEADS = 8
MOBA_BLOCK = 256
MOBA_TOPK = 3

N_GROUPS = 4
EXPERTS_PER_GROUP = 8
N_EXPERTS = 32
MOE_BLOCK = 256

LANES = 128
VMEM_LIMIT = 56 * 1024 * 1024

_NT = (((1,), (1,)), ((), ()))


def _params(n_axes, vmem=VMEM_LIMIT):
    return pltpu.CompilerParams(dimension_semantics=("arbitrary",) * n_axes, vmem_limit_bytes=vmem)


def _rmsnorm_kernel(x_ref, g_ref, o_ref):
    x = x_ref[...]
    ms = jnp.mean(x * x, axis=-1, keepdims=True)
    o_ref[...] = (x * lax.rsqrt(ms + RMS_EPS) * g_ref[...]).astype(o_ref.dtype)


def rmsnorm_cast(x2d, gain, tm=512):
    t, d = x2d.shape
    return pl.pallas_call(
        _rmsnorm_kernel,
        grid=(t // tm,),
        in_specs=[pl.BlockSpec((tm, d), lambda i: (i, 0)), pl.BlockSpec((1, d), lambda i: (0, 0))],
        out_specs=pl.BlockSpec((tm, d), lambda i: (i, 0)),
        out_shape=jax.ShapeDtypeStruct((t, d), BF16),
        compiler_params=_params(1),
        name="rmsnorm",
    )(x2d, gain.reshape(1, d))


IN_PROJ_TN = 256


def _in_proj_kernel(tile_map_ref, a_ref, w_ref, o_ref):
    del tile_map_ref
    o_ref[...] = jnp.dot(a_ref[...], w_ref[...].astype(BF16), preferred_element_type=F32).astype(o_ref.dtype)


def in_proj(h, w_in, layer, tile_map, tm=2048):
    t, k = h.shape
    n = w_in.shape[2]
    tn = IN_PROJ_TN
    tm = min(tm, t)
    grid_spec = pltpu.PrefetchScalarGridSpec(
        num_scalar_prefetch=1,
        grid=(t // tm, n // tn),
        in_specs=[pl.BlockSpec((tm, k), lambda i, j, tmap: (i, 0)),
                  pl.BlockSpec((None, k, tn), lambda i, j, tmap: (layer, 0, j))],
        out_specs=pl.BlockSpec((tm, tn), lambda i, j, tmap: (i, tmap[j])),
    )
    return pl.pallas_call(
        _in_proj_kernel,
        grid_spec=grid_spec,
        out_shape=jax.ShapeDtypeStruct((t, n), BF16),
        compiler_params=_params(2),
        name="in_proj",
    )(tile_map, h, w_in)


def _norm_rope(x, gain, cos, sin_signed, blockdiag, first_half):
    x2 = x * x
    hi = x2.astype(BF16)
    lo = (x2 - hi.astype(F32)).astype(BF16)
    ss = jnp.dot(hi, blockdiag, preferred_element_type=F32) + jnp.dot(lo, blockdiag, preferred_element_type=F32)
    y = x * lax.rsqrt(ss * (1.0 / HEAD_DIM) + RMS_EPS) * gain
    rot = jnp.where(first_half, pltpu.roll(y, LANES - HEAD_DIM // 2, 1), pltpu.roll(y, HEAD_DIM // 2, 1))
    return y * cos + rot * sin_signed


def _prep_kernel(qb_ref, kb_ref, qm_ref, km_ref, cos_ref, sin_ref, gains_ref,
                 qb_o, kb_o, qm_o, km_o, kmean_o, *, scale):
    tm = qb_ref.shape[0]
    cos = cos_ref[...]
    sin = sin_ref[...]
    lane = lax.broadcasted_iota(jnp.int32, (tm, LANES), 1)
    first_half = (lane % HEAD_DIM) < (HEAD_DIM // 2)
    r = lax.broadcasted_iota(jnp.int32, (LANES, LANES), 0) // HEAD_DIM
    c = lax.broadcasted_iota(jnp.int32, (LANES, LANES), 1) // HEAD_DIM
    blockdiag = (r == c).astype(BF16)

    def run(src, dst, gain_row, mult, n_groups):
        outs = []
        gain = gains_ref[gain_row:gain_row + 1, :]
        for g in range(n_groups):
            sl = slice(g * LANES, (g + 1) * LANES)
            y = _norm_rope(src[:, sl].astype(F32), gain, cos, sin, blockdiag, first_half)
            if mult != 1.0:
                y = y * mult
            dst[:, sl] = y.astype(dst.dtype)
            outs.append(y)
        return outs

    run(qb_ref, qb_o, 0, scale, qb_ref.shape[1] // LANES)
    run(kb_ref, kb_o, 1, 1.0, kb_ref.shape[1] // LANES)
    run(qm_ref, qm_o, 2, scale, qm_ref.shape[1] // LANES)
    kms = run(km_ref, km_o, 3, 1.0, km_ref.shape[1] // LANES)
    for g, y in enumerate(kms):
        for blk in range(tm // MOBA_BLOCK):
            s = jnp.sum(y[blk * MOBA_BLOCK:(blk + 1) * MOBA_BLOCK, :], axis=0, keepdims=True)
            kmean_o[blk, :, g * LANES:(g + 1) * LANES] = s * (1.0 / MOBA_BLOCK)


def qk_prep(proj, cols, cos_t, sin_t, gains, seq, tm=512):
    t = proj.shape[0]
    tiles_per_seq = seq // tm
    wqb, wkb, wqm, wkm = SWA_HQ * HEAD_DIM, SWA_HKV * HEAD_DIM, MOBA_HEADS * HEAD_DIM, MOBA_HEADS * HEAD_DIM

    def col_spec(width, off):
        return pl.BlockSpec((tm, width), lambda i, o=off // width: (i, o))

    tab_spec = pl.BlockSpec((tm, LANES), lambda i: (i % tiles_per_seq, 0))
    return pl.pallas_call(
        functools.partial(_prep_kernel, scale=HEAD_DIM ** -0.5),
        grid=(t // tm,),
        in_specs=[col_spec(wqb, cols["qb"]), col_spec(wkb, cols["kb"]), col_spec(wqm, cols["qm"]),
                  col_spec(wkm, cols["km"]), tab_spec, tab_spec, pl.BlockSpec((8, LANES), lambda i: (0, 0))],
        out_specs=[pl.BlockSpec((tm, wqb), lambda i: (i, 0)), pl.BlockSpec((tm, wkb), lambda i: (i, 0)),
                   pl.BlockSpec((tm, wqm), lambda i: (i, 0)), pl.BlockSpec((tm, wkm), lambda i: (i, 0)),
                   pl.BlockSpec((tm // MOBA_BLOCK, 1, wkm), lambda i: (i, 0, 0))],
        out_shape=[jax.ShapeDtypeStruct((t, wqb), BF16), jax.ShapeDtypeStruct((t, wkb), BF16),
                   jax.ShapeDtypeStruct((t, wqm), BF16), jax.ShapeDtypeStruct((t, wkm), BF16),
                   jax.ShapeDtypeStruct((t // MOBA_BLOCK, 1, wkm), F32)],
        compiler_params=_params(1),
        name="qk_prep",
    )(proj, proj, proj, proj, cos_t, sin_t, gains)


def _pool_kernel(cur_ref, prev_ref, pw_ref, ps_ref, o_ref, *, tiles_per_seq):
    tm = cur_ref.shape[0]
    i = pl.program_id(0)
    tile_in_seq = i % tiles_per_seq
    row = lax.broadcasted_iota(jnp.int32, (tm, tm), 0)
    col = lax.broadcasted_iota(jnp.int32, (tm, tm), 1)
    pos = tile_in_seq * tm + lax.broadcasted_iota(jnp.int32, (tm, 1), 0)
    has_prev = (tile_in_seq > 0).astype(F32)
    for g, w in enumerate(POOL_WINDOWS):
        sl = slice(g * POOL_GC, (g + 1) * POOL_GC)
        cur = cur_ref[:, sl]
        prev = prev_ref[:, sl]
        band_cur = ((row >= col) & (row - col < w)).astype(BF16)
        band_prev = (row + tm - col < w).astype(BF16)
        win = jnp.dot(band_cur, cur, preferred_element_type=F32)
        win = win + has_prev * jnp.dot(band_prev, prev, preferred_element_type=F32)
        cnt = jnp.minimum(pos + 1, w).astype(F32)
        pooled = (win / cnt - cur.astype(F32)).astype(BF16)
        y = jnp.dot(pooled, pw_ref[g], preferred_element_type=F32) * ps_ref[:, sl]
        o_ref[:, sl] = y.astype(o_ref.dtype)


def pool_mixer(proj, col_off, pool_w, pool_scale, seq, tm=256):
    t = proj.shape[0]
    width = len(POOL_WINDOWS) * POOL_GC
    cb = col_off // width
    tiles_per_seq = seq // tm
    return pl.pallas_call(
        functools.partial(_pool_kernel, tiles_per_seq=tiles_per_seq),
        grid=(t // tm,),
        in_specs=[pl.BlockSpec((tm, width), lambda i: (i, cb)),
                  pl.BlockSpec((tm, width), lambda i: (jnp.maximum(i - 1, 0), cb)),
                  pl.BlockSpec((len(POOL_WINDOWS), POOL_GC, POOL_GC), lambda i: (0, 0, 0)),
                  pl.BlockSpec((1, width), lambda i: (0, 0))],
        out_specs=pl.BlockSpec((tm, width), lambda i: (i, 0)),
        out_shape=jax.ShapeDtypeStruct((t, width), BF16),
        compiler_params=_params(1),
        name="pool_mixer",
    )(proj, proj, pool_w.astype(BF16), pool_scale.reshape(1, width))


def _swa_kernel(q_ref, kc_ref, kp_ref, vc_ref, vp_ref, sink_ref, o_ref, *, blocks_per_seq):
    blk = SWA_BLOCK
    nk = 2 * blk
    rep = SWA_HQ // SWA_HKV
    ppg = rep // 2
    i = pl.program_id(0)
    not_first = (i % blocks_per_seq) > 0
    lane = lax.broadcasted_iota(jnp.int32, (nk, LANES), 1)

    kk = jnp.concatenate([kp_ref[...], kc_ref[...]], axis=0).astype(F32)
    k0_lo = jnp.where(lane < HEAD_DIM, kk, 0.0)
    k1_hi = jnp.where(lane >= HEAD_DIM, kk, 0.0)
    kmats = ((k0_lo.astype(BF16), pltpu.roll(k0_lo, HEAD_DIM, 1).astype(BF16)),
             (pltpu.roll(k1_hi, HEAD_DIM, 1).astype(BF16), k1_hi.astype(BF16)))
    vt = jnp.concatenate([vp_ref[...], vc_ref[...]], axis=0).astype(F32).T
    ones_rows = (lax.broadcasted_iota(jnp.int32, (VT_ROWS - HEAD_DIM, nk), 0) == 0).astype(F32)
    vts = [jnp.concatenate([vt[g * HEAD_DIM:(g + 1) * HEAD_DIM], ones_rows], axis=0).astype(BF16)
           for g in range(SWA_HKV)]
    qg = [jnp.concatenate([q_ref[:, (g * ppg + pp) * LANES:(g * ppg + pp + 1) * LANES] for pp in range(ppg)], axis=0)
          for g in range(SWA_HKV)]

    ncol = ppg * blk
    key = lax.broadcasted_iota(jnp.int32, (nk, ncol), 0)
    col = lax.broadcasted_iota(jnp.int32, (nk, ncol), 1)
    dist = col % blk + blk - key
    band = (dist >= 0) & (dist < blk) & (not_first | (key >= blk))
    seg = lax.broadcasted_iota(jnp.int32, (1, ncol), 1) // blk

    tasks = [(g, par) for g in range(SWA_HKV) for par in range(2)]

    def scores(g, par):
        return lax.dot_general(kmats[g][par], qg[g], _NT, preferred_element_type=F32)

    outs = {}
    st_next = scores(*tasks[0])
    for n, (g, par) in enumerate(tasks):
        st = st_next
        if n + 1 < len(tasks):
            st_next = scores(*tasks[n + 1])
        sink = jnp.zeros((1, ncol), F32)
        for pp in range(ppg):
            sink = jnp.where(seg == pp, sink_ref[0, g * rep + 2 * pp + par], sink)
        sb = jnp.where(band, st, NEG_INF).astype(BF16)
        mb = jnp.maximum(jnp.max(sb, axis=0, keepdims=True), sink.astype(BF16))
        pm = jnp.exp(sb - mb)
        pv = jnp.dot(vts[g], pm, preferred_element_type=F32)
        den = pv[HEAD_DIM:HEAD_DIM + 1] + jnp.exp(sink - mb.astype(F32))
        outs[(g, par)] = pv[0:HEAD_DIM] / den
    for p in range(SWA_HQ // 2):
        g, pp = p // ppg, p % ppg
        pair_t = jnp.concatenate([outs[(g, 0)][:, pp * blk:(pp + 1) * blk], outs[(g, 1)][:, pp * blk:(pp + 1) * blk]],
                                 axis=0)
        o_ref[:, p * LANES:(p + 1) * LANES] = pair_t.T.astype(o_ref.dtype)


def swa_attention(q, k, proj, v_col, sinks, seq):
    t = q.shape[0]
    blk = SWA_BLOCK
    bps = seq // blk
    kvw = SWA_HKV * HEAD_DIM
    vb = v_col // kvw

    def prev(i):
        return jnp.maximum(i - 1, 0)

    return pl.pallas_call(
        functools.partial(_swa_kernel, blocks_per_seq=bps),
        grid=(t // blk,),
        in_specs=[pl.BlockSpec((blk, q.shape[1]), lambda i: (i, 0)),
                  pl.BlockSpec((blk, kvw), lambda i: (i, 0)),
                  pl.BlockSpec((blk, kvw), lambda i: (prev(i), 0)),
                  pl.BlockSpec((blk, kvw), lambda i: (i, vb)),
                  pl.BlockSpec((blk, kvw), lambda i: (prev(i), vb)),
                  pl.BlockSpec((1, SWA_HQ), lambda i: (0, 0), memory_space=pltpu.SMEM)],
        out_specs=pl.BlockSpec((blk, q.shape[1]), lambda i: (i, 0)),
        out_shape=jax.ShapeDtypeStruct(q.shape, BF16),
        compiler_params=_params(1),
        name="swa_attention",
    )(q, k, k, proj, proj, sinks.reshape(1, SWA_HQ).astype(F32))


VT_ROWS = HEAD_DIM + 16
MOBA_TASK_ROWS = 512
MOBA_TASK_COLS = 512
MOBA_ITER_KEYS = 2048


def _moba_kernel(q_ref, k_ref, v_ref, km_ref, o_ref, vt_ref, ke_ref, ko_ref, m_ref, acc_ref,
                 *, nblk, qt):
    blk = MOBA_BLOCK
    sub = qt // blk
    npair = q_ref.shape[1] // LANES
    qi = pl.program_id(2)
    lane = lax.broadcasted_iota(jnp.int32, (blk, LANES), 1)
    low = lane < HEAD_DIM

    @pl.when(qi == 0)
    def _():
        ones_row = (lax.broadcasted_iota(jnp.int32, (VT_ROWS - HEAD_DIM, vt_ref.shape[2]), 0) == 0).astype(BF16)
        for h in range(2 * npair):
            vt_ref[h, HEAD_DIM:VT_ROWS, :] = ones_row

        def fill(j, c):
            off = pl.multiple_of(j * blk, blk)
            vt = v_ref[pl.ds(off, blk), :].astype(F32).T.astype(BF16)
            for h in range(2 * npair):
                vt_ref[h, 0:HEAD_DIM, pl.ds(off, blk)] = vt[h * HEAD_DIM:(h + 1) * HEAD_DIM, :]
            for p in range(npair):
                kf = k_ref[pl.ds(off, blk), p * LANES:(p + 1) * LANES].astype(F32)
                ke_ref[p, pl.ds(off, blk), :] = jnp.where(low, kf, (lane - HEAD_DIM == j).astype(F32)).astype(BF16)
                ko_ref[p, pl.ds(off, blk), :] = jnp.where(low, (lane == j).astype(F32), kf).astype(BF16)
            return c
        lax.fori_loop(0, nblk, fill, 0)

    ncol = npair * qt
    block_of_row = lax.broadcasted_iota(jnp.int32, (HEAD_DIM, ncol), 0)
    own = qi * sub + (lax.broadcasted_iota(jnp.int32, (1, ncol), 1) % qt) // blk

    def block_bias(gates_t):
        g = jnp.where(block_of_row < own, gates_t, -jnp.inf)
        sel = block_of_row == own
        for _ in range(MOBA_TOPK):
            mx = jnp.max(g, axis=0, keepdims=True)
            idx = jnp.min(jnp.where(g == mx, block_of_row, 1 << 20), axis=0, keepdims=True)
            hit = (block_of_row == idx) & (mx > -jnp.inf)
            sel = sel | hit
            g = jnp.where(hit, -jnp.inf, g)
        return jnp.where(sel, 0.0, NEG_INF)

    gates_t = jnp.concatenate(
        [lax.dot_general(km_ref[p], q_ref[:, p * LANES:(p + 1) * LANES], _NT, preferred_element_type=F32)
         for p in range(npair)], axis=1)
    bias_t = jnp.concatenate([block_bias(gates_t[0:HEAD_DIM]), block_bias(gates_t[HEAD_DIM:])], axis=0)
    lane_q = lax.broadcasted_iota(jnp.int32, (qt, LANES), 1)
    low_q = lane_q < HEAD_DIM
    q_ext = []
    for p in range(npair):
        bias = bias_t[:, p * qt:(p + 1) * qt].T
        qf = q_ref[:, p * LANES:(p + 1) * LANES].astype(F32)
        q_ext.append((jnp.where(low_q, qf, bias).astype(BF16),
                      jnp.where(low_q, bias, qf).astype(BF16)))

    m_ref[...] = jnp.full(m_ref.shape, -jnp.inf, F32)
    acc_ref[...] = jnp.zeros(acc_ref.shape, F32)
    rows = sub * blk

    def steps(first_group, n_groups, trows, causal):
        start = first_group * rows
        tcols = MOBA_TASK_COLS
        tasks = [(pl.multiple_of(start + part * trows, trows), part, h, ch * tcols)
                 for part in range(n_groups * rows // trows) for h in range(2 * npair) for ch in range(qt // tcols)]

        def scores(off, part, h, c0):
            k_ext = (ke_ref if h % 2 == 0 else ko_ref)[h // 2, pl.ds(off, trows), :]
            return lax.dot_general(k_ext, q_ext[h // 2][h % 2][c0:c0 + tcols], _NT,
                                   preferred_element_type=F32)

        ahead = 1
        pending = [scores(*tasks[n]) for n in range(min(ahead, len(tasks)))]
        for n, (off, part, h, c0) in enumerate(tasks):
            st = pending.pop(0)
            if n + ahead < len(tasks):
                pending.append(scores(*tasks[n + ahead]))
            if causal:
                key_idx = lax.broadcasted_iota(jnp.int32, (trows, tcols), 0) + part * trows
                st = jnp.where(key_idx <= lax.broadcasted_iota(jnp.int32, (trows, tcols), 1) + c0, st, NEG_INF)
            sb = st.astype(BF16)
            m = m_ref[h, :, c0:c0 + tcols]
            m_new = jnp.maximum(m, jnp.max(sb, axis=0, keepdims=True).astype(F32))
            alpha = jnp.exp(m - m_new)
            pm = jnp.exp(sb - m_new.astype(BF16))
            pv = jnp.dot(vt_ref[h, :, pl.ds(off, trows)], pm, preferred_element_type=F32)
            acc_ref[h, :, c0:c0 + tcols] = acc_ref[h, :, c0:c0 + tcols] * alpha + pv
            m_ref[h, :, c0:c0 + tcols] = m_new

    gpi = MOBA_ITER_KEYS // rows

    def body(u, c):
        steps(gpi * u, gpi, MOBA_TASK_ROWS, False)
        return c

    lax.fori_loop(0, qi // gpi, body, 0)

    def tail(jt, c):
        steps(jt, 1, MOBA_TASK_ROWS, False)
        return c

    lax.fori_loop((qi // gpi) * gpi, qi, tail, 0)

    steps(qi, 1, MOBA_TASK_ROWS, True)
    outs = [acc_ref[h, 0:HEAD_DIM, :] / acc_ref[h, HEAD_DIM:HEAD_DIM + 1, :] for h in range(2 * npair)]
    o_ref[...] = jnp.concatenate(outs, axis=0).T.astype(o_ref.dtype)


def moba_attention(q, k, proj, v_col, gate_mats, batch, seq, qt=512, npair=2):
    t, width = q.shape
    nblk = seq // MOBA_BLOCK
    qt = min(qt, seq)
    gw = npair * LANES
    q3 = q.reshape(batch, seq, width)
    k3 = k.reshape(batch, seq, width)
    p3 = proj.reshape(batch, seq, proj.shape[1])
    vb = v_col // gw

    def resident(shape, index_map):
        return pl.BlockSpec(shape, index_map, pipeline_mode=pl.Buffered(1))

    out = pl.pallas_call(
        functools.partial(_moba_kernel, nblk=nblk, qt=qt),
        grid=(batch, width // gw, seq // qt),
        in_specs=[pl.BlockSpec((None, qt, gw), lambda b, g, i: (b, i, g)),
                  resident((None, seq, gw), lambda b, g, i: (b, 0, g)),
                  resident((None, seq, gw), lambda b, g, i: (b, 0, vb + g)),
                  pl.BlockSpec((None, npair, LANES, LANES), lambda b, g, i: (b, g, 0, 0))],
        out_specs=pl.BlockSpec((None, qt, gw), lambda b, g, i: (b, i, g)),
        out_shape=jax.ShapeDtypeStruct((batch, seq, width), BF16),
        scratch_shapes=[pltpu.VMEM((2 * npair, VT_ROWS, seq), BF16),
                        pltpu.VMEM((npair, seq, LANES), BF16), pltpu.VMEM((npair, seq, LANES), BF16),
                        pltpu.VMEM((2 * npair, 1, qt), F32),
                        pltpu.VMEM((2 * npair, VT_ROWS, qt), F32)],
        compiler_params=_params(3),
        name="moba_attention",
    )(q3, k3, p3, gate_mats)
    return out.reshape(t, width)


def moba_gate_mats(kmean, batch, nblk):
    km = kmean.reshape(batch, nblk, MOBA_HEADS // 2, 2, HEAD_DIM).astype(BF16)
    km = jnp.transpose(km, (0, 2, 3, 1, 4))
    pad_rows = LANES // 2 - nblk
    even = jnp.pad(km[:, :, 0], ((0, 0), (0, 0), (LANES // 2, pad_rows), (0, HEAD_DIM)))
    odd = jnp.pad(km[:, :, 1], ((0, 0), (0, 0), (0, LANES // 2 + pad_rows), (HEAD_DIM, 0)))
    return even + odd


def _outproj_kernel(x_ref, ya_ref, yb_ref, yc_ref, ga_ref, gb_ref, gc_ref,
                    wa_ref, wb_ref, wc_ref, wo_ref, gain_ref, wr_ref, xo_ref, h_ref, logits_ref):
    def gate(g_ref):
        return jax.nn.sigmoid(g_ref[...].astype(F32))

    merged = gate(ga_ref) * jnp.dot(ya_ref[...], wa_ref[...], preferred_element_type=F32)
    merged = merged + gate(gb_ref) * jnp.dot(yb_ref[...], wb_ref[...], preferred_element_type=F32)
    merged = merged + gate(gc_ref) * jnp.dot(yc_ref[...], wc_ref[...], preferred_element_type=F32)
    xn = x_ref[...] + jnp.dot(merged.astype(BF16), wo_ref[...], preferred_element_type=F32)
    xo_ref[...] = xn
    ms = jnp.mean(xn * xn, axis=-1, keepdims=True)
    h = xn * lax.rsqrt(ms + RMS_EPS) * gain_ref[...]
    h_ref[...] = h
    logits_ref[...] = lax.dot_general(wr_ref[...], h.astype(BF16), _NT, preferred_element_type=F32)


def out_proj(x2d, ya, yb, yc, proj, gate_col, wa, wb, wc, wo, gain, wr_t, tm=256):
    t, d = x2d.shape
    gb0 = gate_col // d

    def rows(width):
        return pl.BlockSpec((tm, width), lambda i: (i, 0))

    def whole(w):
        return pl.BlockSpec(w.shape, lambda i: (0, 0), pipeline_mode=pl.Buffered(1))

    return pl.pallas_call(
        _outproj_kernel,
        grid=(t // tm,),
        in_specs=[rows(d), rows(ya.shape[1]), rows(yb.shape[1]), rows(yc.shape[1]),
                  pl.BlockSpec((tm, d), lambda i: (i, gb0)),
                  pl.BlockSpec((tm, d), lambda i: (i, gb0 + 1)),
                  pl.BlockSpec((tm, d), lambda i: (i, gb0 + 2)),
                  whole(wa), whole(wb), whole(wc), whole(wo),
                  pl.BlockSpec((1, d), lambda i: (0, 0)), whole(wr_t)],
        out_specs=[rows(d), rows(d), pl.BlockSpec((LANES, tm), lambda i: (0, i))],
        out_shape=[jax.ShapeDtypeStruct((t, d), F32), jax.ShapeDtypeStruct((t, d), F32),
                   jax.ShapeDtypeStruct((LANES, t), F32)],
        compiler_params=_params(1),
        name="out_proj",
    )(x2d, ya, yb, yc, proj, proj, proj, wa, wb, wc, wo, gain.reshape(1, d), wr_t)


def _router_kernel(logits_ref, route_ref, count_ref, run_ref):
    tm = logits_ref.shape[1]
    epg = EXPERTS_PER_GROUP

    @pl.when(pl.program_id(0) == 0)
    def _():
        run_ref[...] = jnp.zeros(run_ref.shape, F32)

    logits = logits_ref[...]
    row8 = lax.broadcasted_iota(jnp.int32, (epg, tm), 0)
    glog = jnp.where(row8 < N_GROUPS, logits[N_EXPERTS:N_EXPERTS + epg, :], -jnp.inf)
    gmax = jnp.max(glog, axis=0, keepdims=True)
    gsel = jnp.min(jnp.where(glog == gmax, row8, epg), axis=0, keepdims=True)
    p_grp = 1.0 / jnp.sum(jnp.exp(glog - gmax), axis=0, keepdims=True)
    within = jnp.zeros((epg, tm), F32)
    for g in range(N_GROUPS):
        within = jnp.where(gsel == g, logits[g * epg:(g + 1) * epg, :], within)
    wmax = jnp.max(within, axis=0, keepdims=True)
    e = jnp.exp(within - wmax)
    prob = e / jnp.sum(e, axis=0, keepdims=True)
    p1 = jnp.max(prob, axis=0, keepdims=True)
    i1 = jnp.min(jnp.where(prob == p1, row8, epg), axis=0, keepdims=True)
    rest = jnp.where(row8 == i1, -jnp.inf, prob)
    p2 = jnp.max(rest, axis=0, keepdims=True)
    i2 = jnp.min(jnp.where(rest == p2, row8, epg), axis=0, keepdims=True)
    norm = p1 + p2
    w1 = p_grp * p1 / norm
    w2 = p_grp * p2 / norm
    e1 = gsel * epg + i1
    e2 = gsel * epg + i2

    row32 = lax.broadcasted_iota(jnp.int32, (N_EXPERTS, tm), 0)
    oh1 = (row32 == e1).astype(F32)
    oh2 = (row32 == e2).astype(F32)
    oh = oh1 + oh2
    before = (lax.broadcasted_iota(jnp.int32, (tm, tm), 0) < lax.broadcasted_iota(jnp.int32, (tm, tm), 1)).astype(BF16)
    prior = jnp.dot(oh.astype(BF16), before, preferred_element_type=F32) + run_ref[:, 0:1]
    r1 = jnp.sum(oh1 * prior, axis=0, keepdims=True)
    r2 = jnp.sum(oh2 * prior, axis=0, keepdims=True)
    run_new = run_ref[...] + jnp.sum(oh, axis=1, keepdims=True)
    run_ref[...] = run_new
    count_ref[...] = run_new
    zero = jnp.zeros((1, tm), F32)
    route_ref[...] = jnp.concatenate([e1.astype(F32), e2.astype(F32), r1, r2, w1, w2, zero, zero], axis=0)


def router(logits, tm=512):
    t = logits.shape[1]
    return pl.pallas_call(
        _router_kernel,
        grid=(t // tm,),
        in_specs=[pl.BlockSpec((LANES, tm), lambda i: (0, i))],
        out_specs=[pl.BlockSpec((8, tm), lambda i: (0, i)), pl.BlockSpec((N_EXPERTS, LANES), lambda i: (0, 0))],
        out_shape=[jax.ShapeDtypeStruct((8, t), F32), jax.ShapeDtypeStruct((N_EXPERTS, LANES), F32)],
        scratch_shapes=[pltpu.VMEM((N_EXPERTS, LANES), F32)],
        compiler_params=_params(1),
        name="router",
    )(logits)


def _row_copy(src_ref, src_row, dst_ref, dst_row, sem):
    return pltpu.make_async_copy(src_ref.at[pl.ds(src_row, 1)], dst_ref.at[pl.ds(dst_row, 1)], sem)


def _dispatch_kernel(dest_ref, h_ref, init_ref, xs_ref, sem):
    del init_ref
    tm = h_ref.shape[0]

    def start(r, c):
        for k in range(2):
            _row_copy(h_ref, r, xs_ref, dest_ref[k, r], sem).start(priority=k)
        return c

    def wait(r, c):
        for k in range(2):
            _row_copy(h_ref, r, xs_ref, dest_ref[k, r], sem).wait()
        return c

    for r in range(tm):
        start(r, 0)
    lax.fori_loop(0, tm, wait, 0, unroll=8)


def dispatch(h, dest, slots_init, tm=256):
    t, d = h.shape
    n_slots = slots_init.shape[0]
    dest3 = dest.reshape(2, t // tm, tm).transpose(1, 0, 2)
    return pl.pallas_call(
        _dispatch_kernel,
        grid=(t // tm,),
        in_specs=[pl.BlockSpec((None, 2, tm), lambda i: (i, 0, 0), memory_space=pltpu.SMEM),
                  pl.BlockSpec((tm, d), lambda i: (i, 0)),
                  pl.BlockSpec(memory_space=pl.ANY)],
        out_specs=pl.BlockSpec(memory_space=pl.ANY),
        out_shape=jax.ShapeDtypeStruct((n_slots, d), h.dtype),
        input_output_aliases={2: 0},
        scratch_shapes=[pltpu.SemaphoreType.DMA],
        compiler_params=_params(1),
        name="moe_dispatch",
    )(dest3, h, slots_init)


CAST_ROWS = 256


def _cast_rows(src_ref, dst_ref):
    def body(c, carry):
        sl = pl.ds(pl.multiple_of(c * CAST_ROWS, CAST_ROWS), CAST_ROWS)
        dst_ref[sl, :] = src_ref[sl, :].astype(dst_ref.dtype)
        return carry
    lax.fori_loop(0, src_ref.shape[0] // CAST_ROWS, body, 0)


def _expert_kernel(bexp_ref, nvalid_ref, next_ref, xs_ref, wg_hbm, wu_hbm, wd_hbm, ys_ref,
                   wg_st, wu_st, wd_st, wg_bf, wu_bf, wd_bf, state_ref, sems, *, layer):
    n = pl.program_id(0)
    nv = nvalid_ref[n]
    e = bexp_ref[n]

    def fetch(expert):
        return [pltpu.make_async_copy(src.at[layer, expert], dst, sems.at[k])
                for k, (src, dst) in enumerate(((wg_hbm, wg_st), (wu_hbm, wu_st), (wd_hbm, wd_st)))]

    @pl.when(n == 0)
    def _():
        state_ref[0] = -1
        state_ref[1] = -1

    @pl.when((nv > 0) & (state_ref[0] != e))
    def _():
        @pl.when(state_ref[1] != e)
        def _():
            for c in fetch(e):
                c.start()
            state_ref[1] = e

        for c in fetch(e):
            c.wait()
        _cast_rows(wg_st, wg_bf)
        _cast_rows(wu_st, wu_bf)
        _cast_rows(wd_st, wd_bf)
        state_ref[0] = e
        nxt = next_ref[n]

        @pl.when(nxt >= 0)
        def _():
            for c in fetch(nxt):
                c.start()
            state_ref[1] = nxt

    @pl.when(nv > 0)
    def _():
        row = lax.broadcasted_iota(jnp.int32, xs_ref.shape, 0)
        x = jnp.where(row < nv, xs_ref[...], 0.0).astype(BF16)
        g = jnp.dot(x, wg_bf[...], preferred_element_type=F32)
        u = jnp.dot(x, wu_bf[...], preferred_element_type=F32)
        act = (jax.nn.silu(g) * u).astype(BF16)
        ys_ref[...] = jnp.dot(act, wd_bf[...], preferred_element_type=F32)

    @pl.when(nv <= 0)
    def _():
        ys_ref[...] = jnp.zeros(ys_ref.shape, ys_ref.dtype)


def experts(xs, blk_exp, nvalid, next_exp, wg, wu, wd, layer):
    n_slots, d = xs.shape
    ff = wg.shape[3]
    nb = n_slots // MOE_BLOCK
    hbm = pl.BlockSpec(memory_space=pl.ANY)
    grid_spec = pltpu.PrefetchScalarGridSpec(
        num_scalar_prefetch=3,
        grid=(nb,),
        in_specs=[pl.BlockSpec((MOE_BLOCK, d), lambda n, be, nv, nx: (n, 0)), hbm, hbm, hbm],
        out_specs=pl.BlockSpec((MOE_BLOCK, d), lambda n, be, nv, nx: (n, 0)),
        scratch_shapes=[pltpu.VMEM((d, ff), F32), pltpu.VMEM((d, ff), F32), pltpu.VMEM((ff, d), F32),
                        pltpu.VMEM((d, ff), BF16), pltpu.VMEM((d, ff), BF16), pltpu.VMEM((ff, d), BF16),
                        pltpu.SMEM((2,), jnp.int32), pltpu.SemaphoreType.DMA((3,))],
    )
    return pl.pallas_call(
        functools.partial(_expert_kernel, layer=layer),
        grid_spec=grid_spec,
        out_shape=jax.ShapeDtypeStruct((n_slots, d), F32),
        compiler_params=_params(1),
        name="moe_experts",
    )(blk_exp, nvalid, next_exp, xs, wg, wu, wd)


def _combine_kernel(dest_ref, dest_next_ref, x_ref, w_ref, gain_ref, ys_ref, xo_ref, *rest):
    h_ref, buf_ref, sems = rest if len(rest) == 3 else (None,) + rest
    tm = x_ref.shape[0]
    i = pl.program_id(0)
    slot = i % 2

    def gather(d_ref, s, wait):
        def body(r, c):
            for k in range(2):
                cp = _row_copy(ys_ref, d_ref[k, r], buf_ref.at[s, k], r, sems.at[s])
                if wait:
                    cp.wait()
                else:
                    cp.start(priority=k)
            return c
        if wait:
            lax.fori_loop(0, tm, body, 0, unroll=8)
        else:
            for r in range(tm):
                body(r, 0)

    @pl.when(i == 0)
    def _():
        gather(dest_ref, 0, wait=False)

    @pl.when(i + 1 < pl.num_programs(0))
    def _():
        gather(dest_next_ref, 1 - slot, wait=False)

    gather(dest_ref, slot, wait=True)
    w = w_ref[...]
    xn = x_ref[...] + buf_ref[slot, 0] * w[:, 0:1] + buf_ref[slot, 1] * w[:, 1:2]
    xo_ref[...] = xn
    if h_ref is not None:
        ms = jnp.mean(xn * xn, axis=-1, keepdims=True)
        h_ref[...] = (xn * lax.rsqrt(ms + RMS_EPS) * gain_ref[...]).astype(h_ref.dtype)


def combine(x2d, ys, dest, wts, gain, emit_norm, tm=256):
    t, d = x2d.shape
    dest3 = dest.reshape(2, t // tm, tm).transpose(1, 0, 2)
    return pl.pallas_call(
        _combine_kernel,
        grid=(t // tm,),
        in_specs=[pl.BlockSpec((None, 2, tm), lambda i: (i, 0, 0), memory_space=pltpu.SMEM),
                  pl.BlockSpec((None, 2, tm), lambda i: (jnp.minimum(i + 1, t // tm - 1), 0, 0),
                               memory_space=pltpu.SMEM),
                  pl.BlockSpec((tm, d), lambda i: (i, 0)),
                  pl.BlockSpec((tm, 2), lambda i: (i, 0)),
                  pl.BlockSpec((1, d), lambda i: (0, 0)),
                  pl.BlockSpec(memory_space=pl.ANY)],
        out_specs=[pl.BlockSpec((tm, d), lambda i: (i, 0))] * (2 if emit_norm else 1),
        out_shape=[jax.ShapeDtypeStruct((t, d), F32)] + ([jax.ShapeDtypeStruct((t, d), BF16)] if emit_norm else []),
        scratch_shapes=[pltpu.VMEM((2, 2, tm, d), F32), pltpu.SemaphoreType.DMA((2,))],
        compiler_params=_params(1),
        name="moe_combine",
    )(dest3, dest3, x2d, wts, gain.reshape(1, d), ys)


def _rope_tables(seq):
    inv_freq = 1.0 / (ROPE_THETA ** (jnp.arange(0, HEAD_DIM, 2, dtype=F32) / HEAD_DIM))
    ang = jnp.arange(seq, dtype=F32)[:, None] * inv_freq[None, :]
    cos, sin = jnp.cos(ang), jnp.sin(ang)
    cos_t = jnp.concatenate([cos, cos, cos, cos], axis=1)
    sin_t = jnp.concatenate([-sin, sin, -sin, sin], axis=1)
    return cos_t, sin_t


def _moe_plan(route, counts, n_tokens):
    eid = route[0:2].astype(jnp.int32)
    rank = route[2:4].astype(jnp.int32)
    wts = route[4:6].T
    cnt = counts[:, 0].astype(jnp.int32)
    padded = ((cnt + MOE_BLOCK - 1) // MOE_BLOCK) * MOE_BLOCK
    p_ends = jnp.cumsum(padded)
    p_starts = p_ends - padded
    hit = eid[..., None] == jnp.arange(N_EXPERTS, dtype=jnp.int32)
    dest = rank + jnp.sum(jnp.where(hit, p_starts, 0), axis=-1)
    n_blocks = -(-(n_tokens * 2) // MOE_BLOCK) + N_EXPERTS
    blk_start = jnp.arange(n_blocks, dtype=jnp.int32) * MOE_BLOCK
    blk_exp = jnp.minimum(jnp.sum(p_ends[None, :] <= blk_start[:, None], axis=1), N_EXPERTS - 1).astype(jnp.int32)
    nvalid = jnp.clip(cnt[blk_exp] - (blk_start - p_starts[blk_exp]), 0, MOE_BLOCK).astype(jnp.int32)
    ids = jnp.arange(N_EXPERTS, dtype=jnp.int32)
    later_used = (ids[None, :] > ids[:, None]) & (cnt[None, :] > 0)
    next_used = jnp.min(jnp.where(later_used, ids[None, :], N_EXPERTS), axis=1)
    next_exp = jnp.where(next_used < N_EXPERTS, next_used, -1)[blk_exp].astype(jnp.int32)
    return dest, wts, blk_exp, nvalid, next_exp, n_blocks * MOE_BLOCK


def kernel(x, norm_mix, w_in, pool_w, pool_scale, swa_q_norm, swa_k_norm, swa_sinks, moba_q_norm, moba_k_norm,
           w_up_a, w_up_b, w_up_c, w_o, norm_ffn, w_router_group, w_router_expert, w_exp_gate, w_exp_up, w_exp_down):
    batch, seq, d = x.shape
    t = batch * seq
    depth = w_in.shape[0]
    pool_width = len(POOL_WINDOWS) * POOL_GC
    wqb, wkv, wm = SWA_HQ * HEAD_DIM, SWA_HKV * HEAD_DIM, MOBA_HEADS * HEAD_DIM

    names = ["xa", "qb", "kb", "vb", "qm", "km", "vm", "ga", "gb", "gc"]
    widths = [pool_width, wqb, wkv, wkv, wm, wm, wm, d, d, d]
    src, o = {}, 0
    for nme, w in zip(names, widths):
        src[nme] = (o, w)
        o += w
    order = ["ga", "gb", "gc", "qb", "qm", "km", "vm", "xa", "kb", "vb"]
    cols, o = {}, 0
    for nme in order:
        cols[nme] = o
        o += src[nme][1]
    tile_map = [0] * (o // IN_PROJ_TN)
    for nme in order:
        if nme == "vb":
            continue
        width = src[nme][1] + (src["vb"][1] if nme == "kb" else 0)
        for k in range(width // IN_PROJ_TN):
            tile_map[src[nme][0] // IN_PROJ_TN + k] = cols[nme] // IN_PROJ_TN + k
    tile_map = jnp.asarray(tile_map, jnp.int32)

    cos_t, sin_t = _rope_tables(seq)
    x2d = x.reshape(t, d)
    h = rmsnorm_cast(x2d, norm_mix[0])
    xs = None

    for l in range(depth):
        proj = in_proj(h, w_in, l, tile_map)

        def tile2(g):
            return jnp.concatenate([g, g], axis=0)

        gains = jnp.stack([tile2(swa_q_norm[l]), tile2(swa_k_norm[l]), tile2(moba_q_norm[l]), tile2(moba_k_norm[l])]
                          + [jnp.zeros((LANES,), F32)] * 4)
        qb, kb, qm, km, kmean = qk_prep(proj, cols, cos_t, sin_t, gains, seq)
        y_a = pool_mixer(proj, cols["xa"], pool_w[l], pool_scale[l], seq)
        y_b = swa_attention(qb, kb, proj, cols["vb"], swa_sinks[l], seq)
        gate_mats = moba_gate_mats(kmean, batch, seq // MOBA_BLOCK)
        y_c = moba_attention(qm, km, proj, cols["vm"], gate_mats, batch, seq)
        wr_t = jnp.concatenate([w_router_expert[l].T, w_router_group[l].T,
                                jnp.zeros((LANES - N_EXPERTS - N_GROUPS, d), F32)], axis=0).astype(BF16)
        x2d, h_ffn, logits = out_proj(x2d, y_a, y_b, y_c, proj, cols["ga"],
                                      w_up_a[l].astype(BF16), w_up_b[l].astype(BF16), w_up_c[l].astype(BF16),
                                      w_o[l].astype(BF16), norm_ffn[l], wr_t)
        route, counts = router(logits)
        dest, wts, blk_exp, nvalid, next_exp, n_slots = _moe_plan(route, counts, t)
        xs = dispatch(h_ffn, dest, jnp.zeros((n_slots, d), F32) if xs is None else xs)
        ys = experts(xs, blk_exp, nvalid, next_exp, w_exp_gate, w_exp_up, w_exp_down, l)
        last = l + 1 == depth
        outs = combine(x2d, ys, dest, wts, norm_mix[0 if last else l + 1], emit_norm=not last)
        x2d = outs[0]
        h = None if last else outs[1]
    return x2d.reshape(batch, seq, d)
```

```python
import functools

import jax
import jax.numpy as jnp
from jax import lax
from jax.experimental import pallas as pl
from jax.experimental.pallas import tpu as pltpu

F32 = jnp.float32
BF16 = jnp.bfloat16

HEAD_DIM = 64
ROPE_THETA = 10000.0
RMS_EPS = 1e-6
NEG_INF = -1e30

POOL_WINDOWS = (2, 4, 8, 16)
POOL_GC = 128

SWA_HQ = 16
SWA_HKV = 2
SWA_BLOCK = 128

MOBA_HEADS = 8
MOBA_BLOCK = 256
MOBA_TOPK = 3

N_GROUPS = 4
EXPERTS_PER_GROUP = 8
N_EXPERTS = 32
MOE_BLOCK = 256

LANES = 128
VMEM_LIMIT = 56 * 1024 * 1024

_NT = (((1,), (1,)), ((), ()))


def _params(n_axes, vmem=VMEM_LIMIT):
    return pltpu.CompilerParams(dimension_semantics=("arbitrary",) * n_axes, vmem_limit_bytes=vmem)


def _rmsnorm_kernel(x_ref, g_ref, o_ref):
    x = x_ref[...]
    ms = jnp.mean(x * x, axis=-1, keepdims=True)
    o_ref[...] = (x * lax.rsqrt(ms + RMS_EPS) * g_ref[...]).astype(o_ref.dtype)


def rmsnorm_cast(x2d, gain, tm=512):
    t, d = x2d.shape
    return pl.pallas_call(
        _rmsnorm_kernel,
        grid=(t // tm,),
        in_specs=[pl.BlockSpec((tm, d), lambda i: (i, 0)), pl.BlockSpec((1, d), lambda i: (0, 0))],
        out_specs=pl.BlockSpec((tm, d), lambda i: (i, 0)),
        out_shape=jax.ShapeDtypeStruct((t, d), BF16),
        compiler_params=_params(1),
        name="rmsnorm",
    )(x2d, gain.reshape(1, d))


IN_PROJ_TN = 256


def _in_proj_kernel(tile_map_ref, a_ref, w_ref, o_ref):
    del tile_map_ref
    o_ref[...] = jnp.dot(a_ref[...], w_ref[...].astype(BF16), preferred_element_type=F32).astype(o_ref.dtype)


def in_proj(h, w_in, layer, tile_map, tm=2048):
    t, k = h.shape
    n = w_in.shape[2]
    tn = IN_PROJ_TN
    tm = min(tm, t)
    grid_spec = pltpu.PrefetchScalarGridSpec(
        num_scalar_prefetch=1,
        grid=(t // tm, n // tn),
        in_specs=[pl.BlockSpec((tm, k), lambda i, j, tmap: (i, 0)),
                  pl.BlockSpec((None, k, tn), lambda i, j, tmap: (layer, 0, j))],
        out_specs=pl.BlockSpec((tm, tn), lambda i, j, tmap: (i, tmap[j])),
    )
    return pl.pallas_call(
        _in_proj_kernel,
        grid_spec=grid_spec,
        out_shape=jax.ShapeDtypeStruct((t, n), BF16),
        compiler_params=_params(2),
        name="in_proj",
    )(tile_map, h, w_in)


def _norm_rope(x, gain, cos, sin_signed, blockdiag, first_half):
    x2 = x * x
    hi = x2.astype(BF16)
    lo = (x2 - hi.astype(F32)).astype(BF16)
    ss = jnp.dot(hi, blockdiag, preferred_element_type=F32) + jnp.dot(lo, blockdiag, preferred_element_type=F32)
    y = x * lax.rsqrt(ss * (1.0 / HEAD_DIM) + RMS_EPS) * gain
    rot = jnp.where(first_half, pltpu.roll(y, LANES - HEAD_DIM // 2, 1), pltpu.roll(y, HEAD_DIM // 2, 1))
    return y * cos + rot * sin_signed


def _prep_kernel(qb_ref, kb_ref, qm_ref, km_ref, cos_ref, sin_ref, gains_ref,
                 qb_o, kb_o, qm_o, km_o, kmean_o, *, scale):
    tm = qb_ref.shape[0]
    cos = cos_ref[...]
    sin = sin_ref[...]
    lane = lax.broadcasted_iota(jnp.int32, (tm, LANES), 1)
    first_half = (lane % HEAD_DIM) < (HEAD_DIM // 2)
    r = lax.broadcasted_iota(jnp.int32, (LANES, LANES), 0) // HEAD_DIM
    c = lax.broadcasted_iota(jnp.int32, (LANES, LANES), 1) // HEAD_DIM
    blockdiag = (r == c).astype(BF16)

    def run(src, dst, gain_row, mult, n_groups):
        outs = []
        gain = gains_ref[gain_row:gain_row + 1, :]
        for g in range(n_groups):
            sl = slice(g * LANES, (g + 1) * LANES)
            y = _norm_rope(src[:, sl].astype(F32), gain, cos, sin, blockdiag, first_half)
            if mult != 1.0:
                y = y * mult
            dst[:, sl] = y.astype(dst.dtype)
            outs.append(y)
        return outs

    run(qb_ref, qb_o, 0, scale, qb_ref.shape[1] // LANES)
    run(kb_ref, kb_o, 1, 1.0, kb_ref.shape[1] // LANES)
    run(qm_ref, qm_o, 2, scale, qm_ref.shape[1] // LANES)
    kms = run(km_ref, km_o, 3, 1.0, km_ref.shape[1] // LANES)
    for g, y in enumerate(kms):
        for blk in range(tm // MOBA_BLOCK):
            s = jnp.sum(y[blk * MOBA_BLOCK:(blk + 1) * MOBA_BLOCK, :], axis=0, keepdims=True)
            kmean_o[blk, :, g * LANES:(g + 1) * LANES] = s * (1.0 / MOBA_BLOCK)


def qk_prep(proj, cols, cos_t, sin_t, gains, seq, tm=512):
    t = proj.shape[0]
    tiles_per_seq = seq // tm
    wqb, wkb, wqm, wkm = SWA_HQ * HEAD_DIM, SWA_HKV * HEAD_DIM, MOBA_HEADS * HEAD_DIM, MOBA_HEADS * HEAD_DIM

    def col_spec(width, off):
        return pl.BlockSpec((tm, width), lambda i, o=off // width: (i, o))

    tab_spec = pl.BlockSpec((tm, LANES), lambda i: (i % tiles_per_seq, 0))
    return pl.pallas_call(
        functools.partial(_prep_kernel, scale=HEAD_DIM ** -0.5),
        grid=(t // tm,),
        in_specs=[col_spec(wqb, cols["qb"]), col_spec(wkb, cols["kb"]), col_spec(wqm, cols["qm"]),
                  col_spec(wkm, cols["km"]), tab_spec, tab_spec, pl.BlockSpec((8, LANES), lambda i: (0, 0))],
        out_specs=[pl.BlockSpec((tm, wqb), lambda i: (i, 0)), pl.BlockSpec((tm, wkb), lambda i: (i, 0)),
                   pl.BlockSpec((tm, wqm), lambda i: (i, 0)), pl.BlockSpec((tm, wkm), lambda i: (i, 0)),
                   pl.BlockSpec((tm // MOBA_BLOCK, 1, wkm), lambda i: (i, 0, 0))],
        out_shape=[jax.ShapeDtypeStruct((t, wqb), BF16), jax.ShapeDtypeStruct((t, wkb), BF16),
                   jax.ShapeDtypeStruct((t, wqm), BF16), jax.ShapeDtypeStruct((t, wkm), BF16),
                   jax.ShapeDtypeStruct((t // MOBA_BLOCK, 1, wkm), F32)],
        compiler_params=_params(1),
        name="qk_prep",
    )(proj, proj, proj, proj, cos_t, sin_t, gains)


def _pool_kernel(cur_ref, prev_ref, pw_ref, ps_ref, o_ref, *, tiles_per_seq):
    tm = cur_ref.shape[0]
    i = pl.program_id(0)
    tile_in_seq = i % tiles_per_seq
    row = lax.broadcasted_iota(jnp.int32, (tm, tm), 0)
    col = lax.broadcasted_iota(jnp.int32, (tm, tm), 1)
    pos = tile_in_seq * tm + lax.broadcasted_iota(jnp.int32, (tm, 1), 0)
    has_prev = (tile_in_seq > 0).astype(F32)
    for g, w in enumerate(POOL_WINDOWS):
        sl = slice(g * POOL_GC, (g + 1) * POOL_GC)
        cur = cur_ref[:, sl]
        prev = prev_ref[:, sl]
        band_cur = ((row >= col) & (row - col < w)).astype(BF16)
        band_prev = (row + tm - col < w).astype(BF16)
        win = jnp.dot(band_cur, cur, preferred_element_type=F32)
        win = win + has_prev * jnp.dot(band_prev, prev, preferred_element_type=F32)
        cnt = jnp.minimum(pos + 1, w).astype(F32)
        pooled = (win / cnt - cur.astype(F32)).astype(BF16)
        y = jnp.dot(pooled, pw_ref[g], preferred_element_type=F32) * ps_ref[:, sl]
        o_ref[:, sl] = y.astype(o_ref.dtype)


def pool_mixer(proj, col_off, pool_w, pool_scale, seq, tm=256):
    t = proj.shape[0]
    width = len(POOL_WINDOWS) * POOL_GC
    cb = col_off // width
    tiles_per_seq = seq // tm
    return pl.pallas_call(
        functools.partial(_pool_kernel, tiles_per_seq=tiles_per_seq),
        grid=(t // tm,),
        in_specs=[pl.BlockSpec((tm, width), lambda i: (i, cb)),
                  pl.BlockSpec((tm, width), lambda i: (jnp.maximum(i - 1, 0), cb)),
                  pl.BlockSpec((len(POOL_WINDOWS), POOL_GC, POOL_GC), lambda i: (0, 0, 0)),
                  pl.BlockSpec((1, width), lambda i: (0, 0))],
        out_specs=pl.BlockSpec((tm, width), lambda i: (i, 0)),
        out_shape=jax.ShapeDtypeStruct((t, width), BF16),
        compiler_params=_params(1),
        name="pool_mixer",
    )(proj, proj, pool_w.astype(BF16), pool_scale.reshape(1, width))


def _swa_kernel(q_ref, kc_ref, kp_ref, vc_ref, vp_ref, sink_ref, o_ref, *, blocks_per_seq):
    blk = SWA_BLOCK
    nk = 2 * blk
    rep = SWA_HQ // SWA_HKV
    ppg = rep // 2
    i = pl.program_id(0)
    not_first = (i % blocks_per_seq) > 0
    lane = lax.broadcasted_iota(jnp.int32, (nk, LANES), 1)

    kk = jnp.concatenate([kp_ref[...], kc_ref[...]], axis=0).astype(F32)
    k0_lo = jnp.where(lane < HEAD_DIM, kk, 0.0)
    k1_hi = jnp.where(lane >= HEAD_DIM, kk, 0.0)
    kmats = ((k0_lo.astype(BF16), pltpu.roll(k0_lo, HEAD_DIM, 1).astype(BF16)),
             (pltpu.roll(k1_hi, HEAD_DIM, 1).astype(BF16), k1_hi.astype(BF16)))
    vt = jnp.concatenate([vp_ref[...], vc_ref[...]], axis=0).astype(F32).T
    ones_rows = (lax.broadcasted_iota(jnp.int32, (VT_ROWS - HEAD_DIM, nk), 0) == 0).astype(F32)
    vts = [jnp.concatenate([vt[g * HEAD_DIM:(g + 1) * HEAD_DIM], ones_rows], axis=0).astype(BF16)
           for g in range(SWA_HKV)]
    qg = [jnp.concatenate([q_ref[:, (g * ppg + pp) * LANES:(g * ppg + pp + 1) * LANES] for pp in range(ppg)], axis=0)
          for g in range(SWA_HKV)]

    ncol = ppg * blk
    key = lax.broadcasted_iota(jnp.int32, (nk, ncol), 0)
    col = lax.broadcasted_iota(jnp.int32, (nk, ncol), 1)
    dist = col % blk + blk - key
    band = (dist >= 0) & (dist < blk) & (not_first | (key >= blk))
    seg = lax.broadcasted_iota(jnp.int32, (1, ncol), 1) // blk

    tasks = [(g, par) for g in range(SWA_HKV) for par in range(2)]

    def scores(g, par):
        return lax.dot_general(kmats[g][par], qg[g], _NT, preferred_element_type=F32)

    outs = {}
    st_next = scores(*tasks[0])
    for n, (g, par) in enumerate(tasks):
        st = st_next
        if n + 1 < len(tasks):
            st_next = scores(*tasks[n + 1])
        sink = jnp.zeros((1, ncol), F32)
        for pp in range(ppg):
            sink = jnp.where(seg == pp, sink_ref[0, g * rep + 2 * pp + par], sink)
        sb = jnp.where(band, st, NEG_INF).astype(BF16)
        mb = jnp.maximum(jnp.max(sb, axis=0, keepdims=True), sink.astype(BF16))
        pm = jnp.exp(sb - mb)
        pv = jnp.dot(vts[g], pm, preferred_element_type=F32)
        den = pv[HEAD_DIM:HEAD_DIM + 1] + jnp.exp(sink - mb.astype(F32))
        outs[(g, par)] = pv[0:HEAD_DIM] / den
    for p in range(SWA_HQ // 2):
        g, pp = p // ppg, p % ppg
        pair_t = jnp.concatenate([outs[(g, 0)][:, pp * blk:(pp + 1) * blk], outs[(g, 1)][:, pp * blk:(pp + 1) * blk]],
                                 axis=0)
        o_ref[:, p * LANES:(p + 1) * LANES] = pair_t.T.astype(o_ref.dtype)


def swa_attention(q, k, proj, v_col, sinks, seq):
    t = q.shape[0]
    blk = SWA_BLOCK
    bps = seq // blk
    kvw = SWA_HKV * HEAD_DIM
    vb = v_col // kvw

    def prev(i):
        return jnp.maximum(i - 1, 0)

    return pl.pallas_call(
        functools.partial(_swa_kernel, blocks_per_seq=bps),
        grid=(t // blk,),
        in_specs=[pl.BlockSpec((blk, q.shape[1]), lambda i: (i, 0)),
                  pl.BlockSpec((blk, kvw), lambda i: (i, 0)),
                  pl.BlockSpec((blk, kvw), lambda i: (prev(i), 0)),
                  pl.BlockSpec((blk, kvw), lambda i: (i, vb)),
                  pl.BlockSpec((blk, kvw), lambda i: (prev(i), vb)),
                  pl.BlockSpec((1, SWA_HQ), lambda i: (0, 0), memory_space=pltpu.SMEM)],
        out_specs=pl.BlockSpec((blk, q.shape[1]), lambda i: (i, 0)),
        out_shape=jax.ShapeDtypeStruct(q.shape, BF16),
        compiler_params=_params(1),
        name="swa_attention",
    )(q, k, k, proj, proj, sinks.reshape(1, SWA_HQ).astype(F32))


VT_ROWS = HEAD_DIM + 16
MOBA_TASK_ROWS = 512
MOBA_TASK_COLS = 512
MOBA_ITER_KEYS = 2048


def _moba_kernel(q_ref, k_ref, v_ref, km_ref, o_ref, vt_ref, ke_ref, ko_ref, m_ref, acc_ref,
                 *, nblk, qt):
    blk = MOBA_BLOCK
    sub = qt // blk
    npair = q_ref.shape[1] // LANES
    qi = pl.program_id(2)
    lane = lax.broadcasted_iota(jnp.int32, (blk, LANES), 1)
    low = lane < HEAD_DIM

    @pl.when(qi == 0)
    def _():
        ones_row = (lax.broadcasted_iota(jnp.int32, (VT_ROWS - HEAD_DIM, vt_ref.shape[2]), 0) == 0).astype(BF16)
        for h in range(2 * npair):
            vt_ref[h, HEAD_DIM:VT_ROWS, :] = ones_row

        def fill(j, c):
            off = pl.multiple_of(j * blk, blk)
            vt = v_ref[pl.ds(off, blk), :].astype(F32).T.astype(BF16)
            for h in range(2 * npair):
                vt_ref[h, 0:HEAD_DIM, pl.ds(off, blk)] = vt[h * HEAD_DIM:(h + 1) * HEAD_DIM, :]
            for p in range(npair):
                kf = k_ref[pl.ds(off, blk), p * LANES:(p + 1) * LANES].astype(F32)
                ke_ref[p, pl.ds(off, blk), :] = jnp.where(low, kf, (lane - HEAD_DIM == j).astype(F32)).astype(BF16)
                ko_ref[p, pl.ds(off, blk), :] = jnp.where(low, (lane == j).astype(F32), kf).astype(BF16)
            return c
        lax.fori_loop(0, nblk, fill, 0)

    ncol = npair * qt
    block_of_row = lax.broadcasted_iota(jnp.int32, (HEAD_DIM, ncol), 0)
    own = qi * sub + (lax.broadcasted_iota(jnp.int32, (1, ncol), 1) % qt) // blk

    def block_bias(gates_t):
        g = jnp.where(block_of_row < own, gates_t, -jnp.inf)
        sel = block_of_row == own
        for _ in range(MOBA_TOPK):
            mx = jnp.max(g, axis=0, keepdims=True)
            idx = jnp.min(jnp.where(g == mx, block_of_row, 1 << 20), axis=0, keepdims=True)
            hit = (block_of_row == idx) & (mx > -jnp.inf)
            sel = sel | hit
            g = jnp.where(hit, -jnp.inf, g)
        return jnp.where(sel, 0.0, NEG_INF)

    gates_t = jnp.concatenate(
        [lax.dot_general(km_ref[p], q_ref[:, p * LANES:(p + 1) * LANES], _NT, preferred_element_type=F32)
         for p in range(npair)], axis=1)
    bias_t = jnp.concatenate([block_bias(gates_t[0:HEAD_DIM]), block_bias(gates_t[HEAD_DIM:])], axis=0)
    lane_q = lax.broadcasted_iota(jnp.int32, (qt, LANES), 1)
    low_q = lane_q < HEAD_DIM
    q_ext = []
    for p in range(npair):
        bias = bias_t[:, p * qt:(p + 1) * qt].T
        qf = q_ref[:, p * LANES:(p + 1) * LANES].astype(F32)
        q_ext.append((jnp.where(low_q, qf, bias).astype(BF16),
                      jnp.where(low_q, bias, qf).astype(BF16)))

    m_ref[...] = jnp.full(m_ref.shape, -jnp.inf, F32)
    acc_ref[...] = jnp.zeros(acc_ref.shape, F32)
    rows = sub * blk

    def steps(first_group, n_groups, trows, causal):
        start = first_group * rows
        tcols = MOBA_TASK_COLS
        tasks = [(pl.multiple_of(start + part * trows, trows), part, h, ch * tcols)
                 for part in range(n_groups * rows // trows) for h in range(2 * npair) for ch in range(qt // tcols)]

        def scores(off, part, h, c0):
            k_ext = (ke_ref if h % 2 == 0 else ko_ref)[h // 2, pl.ds(off, trows), :]
            return lax.dot_general(k_ext, q_ext[h // 2][h % 2][c0:c0 + tcols], _NT,
                                   preferred_element_type=F32)

        ahead = 1
        pending = [scores(*tasks[n]) for n in range(min(ahead, len(tasks)))]
        for n, (off, part, h, c0) in enumerate(tasks):
            st = pending.pop(0)
            if n + ahead < len(tasks):
                pending.append(scores(*tasks[n + ahead]))
            if causal:
                key_idx = lax.broadcasted_iota(jnp.int32, (trows, tcols), 0) + part * trows
                st = jnp.where(key_idx <= lax.broadcasted_iota(jnp.int32, (trows, tcols), 1) + c0, st, NEG_INF)
            sb = st.astype(BF16)
            m = m_ref[h, :, c0:c0 + tcols]
            m_new = jnp.maximum(m, jnp.max(sb, axis=0, keepdims=True).astype(F32))
            alpha = jnp.exp(m - m_new)
            pm = jnp.exp(sb - m_new.astype(BF16))
            pv = jnp.dot(vt_ref[h, :, pl.ds(off, trows)], pm, preferred_element_type=F32)
            acc_ref[h, :, c0:c0 + tcols] = acc_ref[h, :, c0:c0 + tcols] * alpha + pv
            m_ref[h, :, c0:c0 + tcols] = m_new

    gpi = MOBA_ITER_KEYS // rows

    def body(u, c):
        steps(gpi * u, gpi, MOBA_TASK_ROWS, False)
        return c

    lax.fori_loop(0, qi // gpi, body, 0)

    def tail(jt, c):
        steps(jt, 1, MOBA_TASK_ROWS, False)
        return c

    lax.fori_loop((qi // gpi) * gpi, qi, tail, 0)

    steps(qi, 1, MOBA_TASK_ROWS, True)
    outs = [acc_ref[h, 0:HEAD_DIM, :] / acc_ref[h, HEAD_DIM:HEAD_DIM + 1, :] for h in range(2 * npair)]
    o_ref[...] = jnp.concatenate(outs, axis=0).T.astype(o_ref.dtype)


def moba_attention(q, k, proj, v_col, gate_mats, batch, seq, qt=512, npair=2):
    t, width = q.shape
    nblk = seq // MOBA_BLOCK
    qt = min(qt, seq)
    gw = npair * LANES
    q3 = q.reshape(batch, seq, width)
    k3 = k.reshape(batch, seq, width)
    p3 = proj.reshape(batch, seq, proj.shape[1])
    vb = v_col // gw

    def resident(shape, index_map):
        return pl.BlockSpec(shape, index_map, pipeline_mode=pl.Buffered(1))

    out = pl.pallas_call(
        functools.partial(_moba_kernel, nblk=nblk, qt=qt),
        grid=(batch, width // gw, seq // qt),
        in_specs=[pl.BlockSpec((None, qt, gw), lambda b, g, i: (b, i, g)),
                  resident((None, seq, gw), lambda b, g, i: (b, 0, g)),
                  resident((None, seq, gw), lambda b, g, i: (b, 0, vb + g)),
                  pl.BlockSpec((None, npair, LANES, LANES), lambda b, g, i: (b, g, 0, 0))],
        out_specs=pl.BlockSpec((None, qt, gw), lambda b, g, i: (b, i, g)),
        out_shape=jax.ShapeDtypeStruct((batch, seq, width), BF16),
        scratch_shapes=[pltpu.VMEM((2 * npair, VT_ROWS, seq), BF16),
                        pltpu.VMEM((npair, seq, LANES), BF16), pltpu.VMEM((npair, seq, LANES), BF16),
                        pltpu.VMEM((2 * npair, 1, qt), F32),
                        pltpu.VMEM((2 * npair, VT_ROWS, qt), F32)],
        compiler_params=_params(3),
        name="moba_attention",
    )(q3, k3, p3, gate_mats)
    return out.reshape(t, width)


def moba_gate_mats(kmean, batch, nblk):
    km = kmean.reshape(batch, nblk, MOBA_HEADS // 2, 2, HEAD_DIM).astype(BF16)
    km = jnp.transpose(km, (0, 2, 3, 1, 4))
    pad_rows = LANES // 2 - nblk
    even = jnp.pad(km[:, :, 0], ((0, 0), (0, 0), (LANES // 2, pad_rows), (0, HEAD_DIM)))
    odd = jnp.pad(km[:, :, 1], ((0, 0), (0, 0), (0, LANES // 2 + pad_rows), (HEAD_DIM, 0)))
    return even + odd


def _outproj_kernel(x_ref, ya_ref, yb_ref, yc_ref, ga_ref, gb_ref, gc_ref,
                    wa_ref, wb_ref, wc_ref, wo_ref, gain_ref, xo_ref, h_ref):
    def gate(g_ref):
        return jax.nn.sigmoid(g_ref[...].astype(F32))

    merged = gate(ga_ref) * jnp.dot(ya_ref[...], wa_ref[...], preferred_element_type=F32)
    merged = merged + gate(gb_ref) * jnp.dot(yb_ref[...], wb_ref[...], preferred_element_type=F32)
    merged = merged + gate(gc_ref) * jnp.dot(yc_ref[...], wc_ref[...], preferred_element_type=F32)
    xn = x_ref[...] + jnp.dot(merged.astype(BF16), wo_ref[...], preferred_element_type=F32)
    xo_ref[...] = xn
    ms = jnp.mean(xn * xn, axis=-1, keepdims=True)
    h_ref[...] = xn * lax.rsqrt(ms + RMS_EPS) * gain_ref[...]


def out_proj(x2d, ya, yb, yc, proj, gate_col, wa, wb, wc, wo, gain, tm=256):
    t, d = x2d.shape
    gb0 = gate_col // d

    def rows(width):
        return pl.BlockSpec((tm, width), lambda i: (i, 0))

    def whole(w):
        return pl.BlockSpec(w.shape, lambda i: (0, 0), pipeline_mode=pl.Buffered(1))

    return pl.pallas_call(
        _outproj_kernel,
        grid=(t // tm,),
        in_specs=[rows(d), rows(ya.shape[1]), rows(yb.shape[1]), rows(yc.shape[1]),
                  pl.BlockSpec((tm, d), lambda i: (i, gb0)),
                  pl.BlockSpec((tm, d), lambda i: (i, gb0 + 1)),
                  pl.BlockSpec((tm, d), lambda i: (i, gb0 + 2)),
                  whole(wa), whole(wb), whole(wc), whole(wo),
                  pl.BlockSpec((1, d), lambda i: (0, 0))],
        out_specs=[rows(d), rows(d)],
        out_shape=[jax.ShapeDtypeStruct((t, d), F32), jax.ShapeDtypeStruct((t, d), F32)],
        compiler_params=_params(1),
        name="out_proj",
    )(x2d, ya, yb, yc, proj, proj, proj, wa, wb, wc, wo, gain.reshape(1, d))


def _router_kernel(h_ref, wr_ref, route_ref, count_ref, run_ref):
    tm = h_ref.shape[0]
    epg = EXPERTS_PER_GROUP

    @pl.when(pl.program_id(0) == 0)
    def _():
        run_ref[...] = jnp.zeros(run_ref.shape, F32)

    logits = lax.dot_general(wr_ref[...], h_ref[...].astype(BF16), _NT, preferred_element_type=F32)
    row8 = lax.broadcasted_iota(jnp.int32, (epg, tm), 0)
    glog = jnp.where(row8 < N_GROUPS, logits[N_EXPERTS:N_EXPERTS + epg, :], -jnp.inf)
    gmax = jnp.max(glog, axis=0, keepdims=True)
    gsel = jnp.min(jnp.where(glog == gmax, row8, epg), axis=0, keepdims=True)
    p_grp = 1.0 / jnp.sum(jnp.exp(glog - gmax), axis=0, keepdims=True)
    within = jnp.zeros((epg, tm), F32)
    for g in range(N_GROUPS):
        within = jnp.where(gsel == g, logits[g * epg:(g + 1) * epg, :], within)
    wmax = jnp.max(within, axis=0, keepdims=True)
    e = jnp.exp(within - wmax)
    prob = e / jnp.sum(e, axis=0, keepdims=True)
    p1 = jnp.max(prob, axis=0, keepdims=True)
    i1 = jnp.min(jnp.where(prob == p1, row8, epg), axis=0, keepdims=True)
    rest = jnp.where(row8 == i1, -jnp.inf, prob)
    p2 = jnp.max(rest, axis=0, keepdims=True)
    i2 = jnp.min(jnp.where(rest == p2, row8, epg), axis=0, keepdims=True)
    norm = p1 + p2
    w1 = p_grp * p1 / norm
    w2 = p_grp * p2 / norm
    e1 = gsel * epg + i1
    e2 = gsel * epg + i2

    row32 = lax.broadcasted_iota(jnp.int32, (N_EXPERTS, tm), 0)
    oh1 = (row32 == e1).astype(F32)
    oh2 = (row32 == e2).astype(F32)
    oh = oh1 + oh2
    before = (lax.broadcasted_iota(jnp.int32, (tm, tm), 0) < lax.broadcasted_iota(jnp.int32, (tm, tm), 1)).astype(BF16)
    prior = jnp.dot(oh.astype(BF16), before, preferred_element_type=F32) + run_ref[:, 0:1]
    r1 = jnp.sum(oh1 * prior, axis=0, keepdims=True)
    r2 = jnp.sum(oh2 * prior, axis=0, keepdims=True)
    run_new = run_ref[...] + jnp.sum(oh, axis=1, keepdims=True)
    run_ref[...] = run_new
    count_ref[...] = run_new
    zero = jnp.zeros((1, tm), F32)
    route_ref[...] = jnp.concatenate([e1.astype(F32), e2.astype(F32), r1, r2, w1, w2, zero, zero], axis=0)


def router(h, wr_t, tm=512):
    t, d = h.shape
    return pl.pallas_call(
        _router_kernel,
        grid=(t // tm,),
        in_specs=[pl.BlockSpec((tm, d), lambda i: (i, 0)), pl.BlockSpec(wr_t.shape, lambda i: (0, 0))],
        out_specs=[pl.BlockSpec((8, tm), lambda i: (0, i)), pl.BlockSpec((N_EXPERTS, LANES), lambda i: (0, 0))],
        out_shape=[jax.ShapeDtypeStruct((8, t), F32), jax.ShapeDtypeStruct((N_EXPERTS, LANES), F32)],
        scratch_shapes=[pltpu.VMEM((N_EXPERTS, LANES), F32)],
        compiler_params=_params(1),
        name="router",
    )(h, wr_t)


def _row_copy(src_ref, src_row, dst_ref, dst_row, sem):
    return pltpu.make_async_copy(src_ref.at[pl.ds(src_row, 1)], dst_ref.at[pl.ds(dst_row, 1)], sem)


def _dispatch_kernel(dest_ref, h_ref, init_ref, xs_ref, sem):
    del init_ref
    tm = h_ref.shape[0]

    def start(r, c):
        for k in range(2):
            _row_copy(h_ref, r, xs_ref, dest_ref[k, r], sem).start(priority=k)
        return c

    def wait(r, c):
        for k in range(2):
            _row_copy(h_ref, r, xs_ref, dest_ref[k, r], sem).wait()
        return c

    for r in range(tm):
        start(r, 0)
    lax.fori_loop(0, tm, wait, 0, unroll=8)


def dispatch(h, dest, slots_init, tm=256):
    t, d = h.shape
    n_slots = slots_init.shape[0]
    dest3 = dest.reshape(2, t // tm, tm).transpose(1, 0, 2)
    return pl.pallas_call(
        _dispatch_kernel,
        grid=(t // tm,),
        in_specs=[pl.BlockSpec((None, 2, tm), lambda i: (i, 0, 0), memory_space=pltpu.SMEM),
                  pl.BlockSpec((tm, d), lambda i: (i, 0)),
                  pl.BlockSpec(memory_space=pl.ANY)],
        out_specs=pl.BlockSpec(memory_space=pl.ANY),
        out_shape=jax.ShapeDtypeStruct((n_slots, d), h.dtype),
        input_output_aliases={2: 0},
        scratch_shapes=[pltpu.SemaphoreType.DMA],
        compiler_params=_params(1),
        name="moe_dispatch",
    )(dest3, h, slots_init)


CAST_ROWS = 256
EXPERT_FF_CHUNKS = 2


def _cast_rows(src_ref, dst_ref):
    def body(c, carry):
        sl = pl.ds(pl.multiple_of(c * CAST_ROWS, CAST_ROWS), CAST_ROWS)
        dst_ref[sl, :] = src_ref[sl, :].astype(dst_ref.dtype)
        return carry
    lax.fori_loop(0, src_ref.shape[0] // CAST_ROWS, body, 0)


def _expert_kernel(bexp_ref, nvalid_ref, next_ref, xs_ref, wg_hbm, wu_hbm, wd_hbm, ys_ref,
                   wg_st, wu_st, wd_st, wg_bf, wu_bf, wd_bf, state_ref, sems, *, layer):
    n = pl.program_id(0)
    nv = nvalid_ref[n]
    e = bexp_ref[n]

    def fetch(expert):
        return [pltpu.make_async_copy(src.at[layer, expert], dst, sems.at[k])
                for k, (src, dst) in enumerate(((wg_hbm, wg_st), (wu_hbm, wu_st), (wd_hbm, wd_st)))]

    @pl.when(n == 0)
    def _():
        state_ref[0] = -1
        state_ref[1] = -1

    @pl.when((nv > 0) & (state_ref[0] != e))
    def _():
        @pl.when(state_ref[1] != e)
        def _():
            for c in fetch(e):
                c.start()
            state_ref[1] = e

        for c in fetch(e):
            c.wait()
        _cast_rows(wg_st, wg_bf)
        _cast_rows(wu_st, wu_bf)
        _cast_rows(wd_st, wd_bf)
        state_ref[0] = e
        nxt = next_ref[n]

        @pl.when(nxt >= 0)
        def _():
            for c in fetch(nxt):
                c.start()
            state_ref[1] = nxt

    @pl.when(nv > 0)
    def _():
        row = lax.broadcasted_iota(jnp.int32, xs_ref.shape, 0)
        x = jnp.where(row < nv, xs_ref[...], 0.0).astype(BF16)
        ff = wg_bf.shape[1]
        fc = ff // EXPERT_FF_CHUNKS
        y = None
        for c in range(EXPERT_FF_CHUNKS):
            sl = slice(c * fc, (c + 1) * fc)
            g = jnp.dot(x, wg_bf[:, sl], preferred_element_type=F32)
            u = jnp.dot(x, wu_bf[:, sl], preferred_element_type=F32)
            act = (jax.nn.silu(g) * u).astype(BF16)
            part = jnp.dot(act, wd_bf[sl, :], preferred_element_type=F32)
            y = part if y is None else y + part
        ys_ref[...] = y

    @pl.when(nv <= 0)
    def _():
        ys_ref[...] = jnp.zeros(ys_ref.shape, ys_ref.dtype)


def experts(xs, blk_exp, nvalid, next_exp, wg, wu, wd, layer):
    n_slots, d = xs.shape
    ff = wg.shape[3]
    nb = n_slots // MOE_BLOCK
    hbm = pl.BlockSpec(memory_space=pl.ANY)
    grid_spec = pltpu.PrefetchScalarGridSpec(
        num_scalar_prefetch=3,
        grid=(nb,),
        in_specs=[pl.BlockSpec((MOE_BLOCK, d), lambda n, be, nv, nx: (n, 0)), hbm, hbm, hbm],
        out_specs=pl.BlockSpec((MOE_BLOCK, d), lambda n, be, nv, nx: (n, 0)),
        scratch_shapes=[pltpu.VMEM((d, ff), F32), pltpu.VMEM((d, ff), F32), pltpu.VMEM((ff, d), F32),
                        pltpu.VMEM((d, ff), BF16), pltpu.VMEM((d, ff), BF16), pltpu.VMEM((ff, d), BF16),
                        pltpu.SMEM((2,), jnp.int32), pltpu.SemaphoreType.DMA((3,))],
    )
    return pl.pallas_call(
        functools.partial(_expert_kernel, layer=layer),
        grid_spec=grid_spec,
        out_shape=jax.ShapeDtypeStruct((n_slots, d), F32),
        compiler_params=_params(1),
        name="moe_experts",
    )(blk_exp, nvalid, next_exp, xs, wg, wu, wd)


def _combine_kernel(dest_ref, dest_next_ref, x_ref, w_ref, gain_ref, ys_ref, xo_ref, *rest):
    h_ref, buf_ref, sems = rest if len(rest) == 3 else (None,) + rest
    tm = x_ref.shape[0]
    i = pl.program_id(0)
    slot = i % 2

    def gather(d_ref, s, wait):
        def body(r, c):
            for k in range(2):
                cp = _row_copy(ys_ref, d_ref[k, r], buf_ref.at[s, k], r, sems.at[s])
                if wait:
                    cp.wait()
                else:
                    cp.start(priority=k)
            return c
        if wait:
            lax.fori_loop(0, tm, body, 0, unroll=8)
        else:
            for r in range(tm):
                body(r, 0)

    @pl.when(i == 0)
    def _():
        gather(dest_ref, 0, wait=False)

    @pl.when(i + 1 < pl.num_programs(0))
    def _():
        gather(dest_next_ref, 1 - slot, wait=False)

    gather(dest_ref, slot, wait=True)
    w = w_ref[...]
    xn = x_ref[...] + buf_ref[slot, 0] * w[:, 0:1] + buf_ref[slot, 1] * w[:, 1:2]
    xo_ref[...] = xn
    if h_ref is not None:
        ms = jnp.mean(xn * xn, axis=-1, keepdims=True)
        h_ref[...] = (xn * lax.rsqrt(ms + RMS_EPS) * gain_ref[...]).astype(h_ref.dtype)


def combine(x2d, ys, dest, wts, gain, emit_norm, tm=256):
    t, d = x2d.shape
    dest3 = dest.reshape(2, t // tm, tm).transpose(1, 0, 2)
    return pl.pallas_call(
        _combine_kernel,
        grid=(t // tm,),
        in_specs=[pl.BlockSpec((None, 2, tm), lambda i: (i, 0, 0), memory_space=pltpu.SMEM),
                  pl.BlockSpec((None, 2, tm), lambda i: (jnp.minimum(i + 1, t // tm - 1), 0, 0),
                               memory_space=pltpu.SMEM),
                  pl.BlockSpec((tm, d), lambda i: (i, 0)),
                  pl.BlockSpec((tm, 2), lambda i: (i, 0)),
                  pl.BlockSpec((1, d), lambda i: (0, 0)),
                  pl.BlockSpec(memory_space=pl.ANY)],
        out_specs=[pl.BlockSpec((tm, d), lambda i: (i, 0))] * (2 if emit_norm else 1),
        out_shape=[jax.ShapeDtypeStruct((t, d), F32)] + ([jax.ShapeDtypeStruct((t, d), BF16)] if emit_norm else []),
        scratch_shapes=[pltpu.VMEM((2, 2, tm, d), F32), pltpu.SemaphoreType.DMA((2,))],
        compiler_params=_params(1),
        name="moe_combine",
    )(dest3, dest3, x2d, wts, gain.reshape(1, d), ys)


def _rope_tables(seq):
    inv_freq = 1.0 / (ROPE_THETA ** (jnp.arange(0, HEAD_DIM, 2, dtype=F32) / HEAD_DIM))
    ang = jnp.arange(seq, dtype=F32)[:, None] * inv_freq[None, :]
    cos, sin = jnp.cos(ang), jnp.sin(ang)
    cos_t = jnp.concatenate([cos, cos, cos, cos], axis=1)
    sin_t = jnp.concatenate([-sin, sin, -sin, sin], axis=1)
    return cos_t, sin_t


def _moe_plan(route, counts, n_tokens):
    eid = route[0:2].astype(jnp.int32)
    rank = route[2:4].astype(jnp.int32)
    wts = route[4:6].T
    cnt = counts[:, 0].astype(jnp.int32)
    padded = ((cnt + MOE_BLOCK - 1) // MOE_BLOCK) * MOE_BLOCK
    p_ends = jnp.cumsum(padded)
    p_starts = p_ends - padded
    hit = eid[..., None] == jnp.arange(N_EXPERTS, dtype=jnp.int32)
    dest = rank + jnp.sum(jnp.where(hit, p_starts, 0), axis=-1)
    n_blocks = -(-(n_tokens * 2) // MOE_BLOCK) + N_EXPERTS
    blk_start = jnp.arange(n_blocks, dtype=jnp.int32) * MOE_BLOCK
    blk_exp = jnp.minimum(jnp.sum(p_ends[None, :] <= blk_start[:, None], axis=1), N_EXPERTS - 1).astype(jnp.int32)
    nvalid = jnp.clip(cnt[blk_exp] - (blk_start - p_starts[blk_exp]), 0, MOE_BLOCK).astype(jnp.int32)
    ids = jnp.arange(N_EXPERTS, dtype=jnp.int32)
    later_used = (ids[None, :] > ids[:, None]) & (cnt[None, :] > 0)
    next_used = jnp.min(jnp.where(later_used, ids[None, :], N_EXPERTS), axis=1)
    next_exp = jnp.where(next_used < N_EXPERTS, next_used, -1)[blk_exp].astype(jnp.int32)
    return dest, wts, blk_exp, nvalid, next_exp, n_blocks * MOE_BLOCK


def kernel(x, norm_mix, w_in, pool_w, pool_scale, swa_q_norm, swa_k_norm, swa_sinks, moba_q_norm, moba_k_norm,
           w_up_a, w_up_b, w_up_c, w_o, norm_ffn, w_router_group, w_router_expert, w_exp_gate, w_exp_up, w_exp_down):
    batch, seq, d = x.shape
    t = batch * seq
    depth = w_in.shape[0]
    pool_width = len(POOL_WINDOWS) * POOL_GC
    wqb, wkv, wm = SWA_HQ * HEAD_DIM, SWA_HKV * HEAD_DIM, MOBA_HEADS * HEAD_DIM

    names = ["xa", "qb", "kb", "vb", "qm", "km", "vm", "ga", "gb", "gc"]
    widths = [pool_width, wqb, wkv, wkv, wm, wm, wm, d, d, d]
    src, o = {}, 0
    for nme, w in zip(names, widths):
        src[nme] = (o, w)
        o += w
    order = ["ga", "gb", "gc", "qb", "qm", "km", "vm", "xa", "kb", "vb"]
    cols, o = {}, 0
    for nme in order:
        cols[nme] = o
        o += src[nme][1]
    tile_map = [0] * (o // IN_PROJ_TN)
    for nme in order:
        if nme == "vb":
            continue
        width = src[nme][1] + (src["vb"][1] if nme == "kb" else 0)
        for k in range(width // IN_PROJ_TN):
            tile_map[src[nme][0] // IN_PROJ_TN + k] = cols[nme] // IN_PROJ_TN + k
    tile_map = jnp.asarray(tile_map, jnp.int32)

    cos_t, sin_t = _rope_tables(seq)
    x2d = x.reshape(t, d)
    h = rmsnorm_cast(x2d, norm_mix[0])
    xs = None

    for l in range(depth):
        proj = in_proj(h, w_in, l, tile_map)

        def tile2(g):
            return jnp.concatenate([g, g], axis=0)

        gains = jnp.stack([tile2(swa_q_norm[l]), tile2(swa_k_norm[l]), tile2(moba_q_norm[l]), tile2(moba_k_norm[l])]
                          + [jnp.zeros((LANES,), F32)] * 4)
        qb, kb, qm, km, kmean = qk_prep(proj, cols, cos_t, sin_t, gains, seq)
        y_a = pool_mixer(proj, cols["xa"], pool_w[l], pool_scale[l], seq)
        y_b = swa_attention(qb, kb, proj, cols["vb"], swa_sinks[l], seq)
        gate_mats = moba_gate_mats(kmean, batch, seq // MOBA_BLOCK)
        y_c = moba_attention(qm, km, proj, cols["vm"], gate_mats, batch, seq)
        x2d, h_ffn = out_proj(x2d, y_a, y_b, y_c, proj, cols["ga"],
                              w_up_a[l].astype(BF16), w_up_b[l].astype(BF16), w_up_c[l].astype(BF16),
                              w_o[l].astype(BF16), norm_ffn[l])

        wr_t = jnp.concatenate([w_router_expert[l].T, w_router_group[l].T,
                                jnp.zeros((LANES - N_EXPERTS - N_GROUPS, d), F32)], axis=0).astype(BF16)
        route, counts = router(h_ffn, wr_t)
        dest, wts, blk_exp, nvalid, next_exp, n_slots = _moe_plan(route, counts, t)
        xs = dispatch(h_ffn, dest, jnp.zeros((n_slots, d), F32) if xs is None else xs)
        ys = experts(xs, blk_exp, nvalid, next_exp, w_exp_gate, w_exp_up, w_exp_down, l)
        last = l + 1 == depth
        outs = combine(x2d, ys, dest, wts, norm_mix[0 if last else l + 1], emit_norm=not last)
        x2d = outs[0]
        h = None if last else outs[1]
    return x2d.reshape(batch, seq, d)
```
